```python
import jax, jax.numpy as jnp
from jax import lax
import numpy as np

D_MODEL = 1024
BATCH = 32
SEQ = 2048
DEPTH = 1
DEC_BATCH = 32
DEC_SEQ = 16
PAST_LEN = 1024

CHUNK = 64
N_MEM = 256
NORM_EPS = 1e-6
Q_BLOCK = 128
GM_CHUNK = 128
GM_GROUPS = 8
GM_GROUP_DIM = D_MODEL // GM_GROUPS
D_GM = GM_GROUPS * GM_GROUP_DIM
MLA_HEADS = 8
Q_LORA = 384
KV_LORA = 256
NOPE_DIM = 128
ROPE_DIM = 64
V_DIM = 128
D_MLA = MLA_HEADS * V_DIM
ROPE_THETA = 10000.0
MLA_SCALE = (NOPE_DIM + ROPE_DIM) ** -0.5
MEM_HEADS = 4
MEM_HEAD_DIM = 256
D_MEM = MEM_HEADS * MEM_HEAD_DIM
MEM_SCALE = MEM_HEAD_DIM ** -0.5
N_BRANCH = 3
D_FF = -(-(8 * D_MODEL) // (3 * 256)) * 256
IN_SIZES = (D_GM, D_GM, Q_LORA, KV_LORA, ROPE_DIM, D_MEM, D_MODEL, D_MODEL, D_MODEL)
IN_SPLITS = tuple(np.cumsum(IN_SIZES)[:-1].tolist())
D_IN = sum(IN_SIZES)

kernel_name = 'hybrid_gmlp_mla_memory_stream_step'


def rmsnorm(x, g):
    xf = x.astype(jnp.float32)
    y = xf * lax.rsqrt(jnp.mean(xf * xf, axis=-1, keepdims=True) + NORM_EPS)
    return (y * g.astype(jnp.float32)).astype(x.dtype)


def apply_rope(x, pos):
    inv = ROPE_THETA ** (-jnp.arange(0, ROPE_DIM, 2, dtype=jnp.float32) / ROPE_DIM)
    ang = pos.astype(jnp.float32)[:, None] * inv[None, :]
    ang = ang.reshape((ang.shape[0],) + (1,) * (x.ndim - 3) + (ang.shape[1],))
    cos, sin = jnp.cos(ang), jnp.sin(ang)
    x1, x2 = jnp.split(x.astype(jnp.float32), 2, axis=-1)
    return jnp.concatenate([x1 * cos - x2 * sin, x1 * sin + x2 * cos], axis=-1).astype(x.dtype)


def chunk_causal_mask(q_pos, k_pos):
    return (k_pos[None, :] // CHUNK) <= (q_pos[:, None] // CHUNK)


def gmlp_spatial(u, v, w_s, b_s):
    B, T, _ = v.shape
    n_chunks = -(-T // GM_CHUNK)
    pad = n_chunks * GM_CHUNK - T
    vp = jnp.pad(v, ((0, 0), (0, pad), (0, 0))).reshape(B, n_chunks, GM_CHUNK, GM_GROUPS, GM_GROUP_DIM)
    causal = jnp.tril(jnp.ones((GM_CHUNK, GM_CHUNK), dtype=bool))
    w = jnp.where(causal[None], w_s, jnp.zeros_like(w_s))
    mixed = jnp.einsum('gpq,bnqgc->bnpgc', w, vp) + b_s.T[None, None, :, :, None]
    mixed = mixed.reshape(B, n_chunks * GM_CHUNK, D_GM)[:, :T]
    return u * mixed


def mla_block(q_nope, q_rope, q_pos, k_nope, k_rope, v, k_pos):
    s = (jnp.einsum('bqhd,bkhd->bhqk', q_nope, k_nope)
         + jnp.einsum('bqhr,bkr->bhqk', q_rope, k_rope)).astype(jnp.float32) * MLA_SCALE
    s = jnp.where(chunk_causal_mask(q_pos, k_pos)[None, None], s, -jnp.inf)
    p = jax.nn.softmax(s, axis=-1).astype(v.dtype)
    return jnp.einsum('bhqk,bkhd->bqhd', p, v)


def mla_attention(q_nope, q_rope, q_pos, k_nope, k_rope, v, k_pos):
    B, T, H, _ = q_nope.shape
    if T % Q_BLOCK:
        return mla_block(q_nope, q_rope, q_pos, k_nope, k_rope, v, k_pos)
    nb = T // Q_BLOCK

    def to_blocks(a):
        return a.reshape((B, nb, Q_BLOCK) + a.shape[2:]).swapaxes(0, 1)

    out = lax.map(lambda blk: mla_block(blk[0], blk[1], blk[2], k_nope, k_rope, v, k_pos),
                  (to_blocks(q_nope), to_blocks(q_rope), q_pos.reshape(nb, Q_BLOCK)))
    return out.swapaxes(0, 1).reshape(B, T, H, V_DIM)


def memory_kv(mem, g, w_kv):
    B = mem.shape[0]
    kv = jnp.einsum('bmd,de->bme', rmsnorm(mem, g), w_kv)
    k, v = jnp.split(kv, 2, axis=-1)
    return (k.reshape(B, N_MEM, MEM_HEADS, MEM_HEAD_DIM), v.reshape(B, N_MEM, MEM_HEADS, MEM_HEAD_DIM))


def memory_attend(qm, mem_k, mem_v):
    B, T, _ = qm.shape
    q = qm.reshape(B, T, MEM_HEADS, MEM_HEAD_DIM)
    s = jnp.einsum('bthd,bmhd->bhtm', q, mem_k).astype(jnp.float32) * MEM_SCALE
    p = jax.nn.softmax(s, axis=-1).astype(mem_v.dtype)
    return jnp.einsum('bhtm,bmhd->bthd', p, mem_v).reshape(B, T, D_MEM)


def mixing_sublayer(x, pos, lp, ckv_past, kr_past, mem_k, mem_v):
    B, T, _ = x.shape
    xn = rmsnorm(x, lp['norm_mix_g'])
    z = jnp.einsum('btd,de->bte', xn, lp['w_in'])
    u, v, cq, ckv, kr, qm, ga, gb, gc = jnp.split(z, IN_SPLITS, axis=-1)
    u = jax.nn.gelu(u)
    v = rmsnorm(jax.nn.gelu(v), lp['gm_norm_g'])
    o_gm = gmlp_spatial(u, v, lp['gm_ws'], lp['gm_bs'])
    q = jnp.einsum('btc,chd->bthd', rmsnorm(cq, lp['q_norm_g']), lp['w_uq'])
    q_nope = q[..., :NOPE_DIM]
    q_rope = apply_rope(q[..., NOPE_DIM:], pos)
    ckv = rmsnorm(ckv, lp['kv_norm_g'])
    kr = apply_rope(kr, pos)
    if ckv_past is None:
        ckv_all, kr_all, k_pos = ckv, kr, pos
    else:
        ckv_all = jnp.concatenate([ckv_past, ckv], axis=1)
        kr_all = jnp.concatenate([kr_past, kr], axis=1)
        k_pos = jnp.arange(ckv_all.shape[1], dtype=jnp.int32)
    k_nope = jnp.einsum('bkc,chd->bkhd', ckv_all, lp['w_uk'])
    v_mla = jnp.einsum('bkc,chd->bkhd', ckv_all, lp['w_uv'])
    o_mla = mla_attention(q_nope, q_rope, pos, k_nope, kr_all, v_mla, k_pos).reshape(B, T, D_MLA)
    o_mem = memory_attend(qm, mem_k, mem_v)
    merged = (jax.nn.sigmoid(ga) * (o_gm @ lp['w_br_gm'])
              + jax.nn.sigmoid(gb) * (o_mla @ lp['w_br_mla'])
              + jax.nn.sigmoid(gc) * (o_mem @ lp['w_br_mem']))
    y = merged @ lp['w_out']
    return x + y, ckv, kr, v


def ffn_sublayer(h, lp):
    hn = rmsnorm(h, lp['norm_ffn_g'])
    a = jax.nn.silu(hn @ lp['ffn_w_gate']) * (hn @ lp['ffn_w_up'])
    return h + a @ lp['ffn_w_down']


def setup_inputs(seed: int = 0) -> dict:
    key = jax.random.key(seed)
    ks = iter(jax.random.split(key, 32))
    nrm = lambda shape, scale: jax.random.normal(next(ks), shape, jnp.float32) * scale
    gain = lambda shape: 1.0 + 0.01 * jax.random.normal(next(ks), shape, jnp.float32)
    L = DEPTH
    return {
        'x_prompt': nrm((BATCH, SEQ, D_MODEL), 1.0),
        'x_sample': nrm((DEC_BATCH, DEC_SEQ, D_MODEL), 1.0),
        'cache_mla_ckv': nrm((L, DEC_BATCH, PAST_LEN, KV_LORA), 1.0),
        'cache_mla_krope': nrm((L, DEC_BATCH, PAST_LEN, ROPE_DIM), 1.0),
        'cache_mem_k': nrm((L, DEC_BATCH, N_MEM, MEM_HEADS, MEM_HEAD_DIM), 1.0),
        'cache_mem_v': nrm((L, DEC_BATCH, N_MEM, MEM_HEADS, MEM_HEAD_DIM), 1.0),
        'mem_prompt': nrm((BATCH, N_MEM, D_MODEL), 1.0),
        'norm_mix_g': gain((L, D_MODEL)),
        'w_in': nrm((L, D_MODEL, D_IN), D_MODEL ** -0.5),
        'gm_norm_g': gain((L, D_GM)),
        'gm_ws': nrm((L, GM_GROUPS, GM_CHUNK, GM_CHUNK), GM_CHUNK ** -0.5),
        'gm_bs': nrm((L, GM_GROUPS, GM_CHUNK), 0.1),
        'mla_q_norm_g': gain((L, Q_LORA)),
        'mla_w_uq': nrm((L, Q_LORA, MLA_HEADS, NOPE_DIM + ROPE_DIM), Q_LORA ** -0.5),
        'mla_kv_norm_g': gain((L, KV_LORA)),
        'mla_w_uk': nrm((L, KV_LORA, MLA_HEADS, NOPE_DIM), KV_LORA ** -0.5),
        'mla_w_uv': nrm((L, KV_LORA, MLA_HEADS, V_DIM), KV_LORA ** -0.5),
        'mem_norm_g': gain((L, D_MODEL)),
        'mem_w_kv': nrm((L, D_MODEL, 2 * D_MEM), D_MODEL ** -0.5),
        'w_br_gm': nrm((L, D_GM, D_MODEL), D_GM ** -0.5),
        'w_br_mla': nrm((L, D_MLA, D_MODEL), D_MLA ** -0.5),
        'w_br_mem': nrm((L, D_MEM, D_MODEL), D_MEM ** -0.5),
        'w_out': nrm((L, D_MODEL, D_MODEL), D_MODEL ** -0.5),
        'norm_ffn_g': gain((L, D_MODEL)),
        'ffn_w_gate': nrm((L, D_MODEL, D_FF), D_MODEL ** -0.5),
        'ffn_w_up': nrm((L, D_MODEL, D_FF), D_MODEL ** -0.5),
        'ffn_w_down': nrm((L, D_FF, D_MODEL), D_FF ** -0.5),
        'final_norm_g': gain((D_MODEL,)),
    }


def reference(x_prompt, x_sample, cache_mla_ckv, cache_mla_krope, cache_mem_k, cache_mem_v, mem_prompt,
              norm_mix_g, w_in, gm_norm_g, gm_ws, gm_bs, mla_q_norm_g, mla_w_uq, mla_kv_norm_g, mla_w_uk,
              mla_w_uv, mem_norm_g, mem_w_kv, w_br_gm, w_br_mla, w_br_mem, w_out, norm_ffn_g,
              ffn_w_gate, ffn_w_up, ffn_w_down, final_norm_g):
    pos_p = jnp.arange(x_prompt.shape[1], dtype=jnp.int32)
    pos_s = PAST_LEN + jnp.arange(x_sample.shape[1], dtype=jnp.int32)
    hp, hs = x_prompt, x_sample
    ckv_p_l, kr_p_l, mk_p_l, mv_p_l, ckv_s_l, kr_s_l, gv_s_l = [], [], [], [], [], [], []
    for l in range(DEPTH):
        lp = {'norm_mix_g': norm_mix_g[l], 'w_in': w_in[l], 'gm_norm_g': gm_norm_g[l],
              'gm_ws': gm_ws[l], 'gm_bs': gm_bs[l], 'q_norm_g': mla_q_norm_g[l], 'w_uq': mla_w_uq[l],
              'kv_norm_g': mla_kv_norm_g[l], 'w_uk': mla_w_uk[l], 'w_uv': mla_w_uv[l],
              'w_br_gm': w_br_gm[l], 'w_br_mla': w_br_mla[l], 'w_br_mem': w_br_mem[l], 'w_out': w_out[l],
              'norm_ffn_g': norm_ffn_g[l], 'ffn_w_gate': ffn_w_gate[l], 'ffn_w_up': ffn_w_up[l],
              'ffn_w_down': ffn_w_down[l]}
        mk_p, mv_p = memory_kv(mem_prompt, mem_norm_g[l], mem_w_kv[l])
        hp, ckv_p, kr_p, _ = mixing_sublayer(hp, pos_p, lp, None, None, mk_p, mv_p)
        hp = ffn_sublayer(hp, lp)
        hs, ckv_s, kr_s, gv_s = mixing_sublayer(hs, pos_s, lp, cache_mla_ckv[l], cache_mla_krope[l],
                                                cache_mem_k[l], cache_mem_v[l])
        hs = ffn_sublayer(hs, lp)
        ckv_p_l.append(ckv_p); kr_p_l.append(kr_p); mk_p_l.append(mk_p); mv_p_l.append(mv_p)
        ckv_s_l.append(ckv_s); kr_s_l.append(kr_s); gv_s_l.append(gv_s)
    y_prompt = rmsnorm(hp, final_norm_g)
    y_sample = rmsnorm(hs, final_norm_g)
    new_mla_ckv_prompt = jnp.stack(ckv_p_l)
    new_mla_krope_prompt = jnp.stack(kr_p_l)
    new_mem_k_prompt = jnp.stack(mk_p_l)
    new_mem_v_prompt = jnp.stack(mv_p_l)
    new_mla_ckv_sample = jnp.stack(ckv_s_l)
    new_mla_krope_sample = jnp.stack(kr_s_l)
    new_gm_v_sample = jnp.stack(gv_s_l)
    return (y_prompt, y_sample, new_mla_ckv_prompt, new_mla_krope_prompt, new_mem_k_prompt,
            new_mem_v_prompt, new_mla_ckv_sample, new_mla_krope_sample, new_gm_v_sample)
```

```python
import functools
import math

import jax
import jax.numpy as jnp
from jax import lax
from jax.experimental import pallas as pl
from jax.experimental.pallas import tpu as pltpu

F32 = jnp.float32
BF16 = jnp.bfloat16

NORM_EPS = 1e-6
CHUNK = 64
GM_CHUNK = 128
GM_GROUPS = 8
MLA_HEADS = 8
NOPE_DIM = 128
ROPE_DIM = 64
V_DIM = 128
ROPE_THETA = 10000.0
MEM_HEADS = 4
MEM_HEAD_DIM = 256
LOG2E = math.log2(math.e)
MLA_QSCALE = (NOPE_DIM + ROPE_DIM) ** -0.5 * LOG2E
MEM_QSCALE = MEM_HEAD_DIM ** -0.5 * LOG2E

LANE = 128
HEAD_PAD = 2 * LANE
VMEM_LIMIT_BYTES = 56 * 1024 * 1024

TOKEN_BLOCK = 256
Q_BLOCK = 256


def _params(n_axes):
    return pltpu.CompilerParams(dimension_semantics=("arbitrary",) * n_axes,
                                vmem_limit_bytes=VMEM_LIMIT_BYTES)


def _resident(shape):
    return pl.BlockSpec(shape, lambda *_: (0,) * len(shape), pipeline_mode=pl.Buffered(1))


def _rows(tm, width):
    return pl.BlockSpec((tm, width), lambda i: (i, 0))


def _dot(a, b):
    return jnp.dot(a, b, preferred_element_type=F32)


def _dot_nt(a, b):
    return lax.dot_general(a, b, (((1,), (1,)), ((), ())), preferred_element_type=F32)


def _rms(x, g):
    return x * lax.rsqrt(jnp.mean(x * x, axis=-1, keepdims=True) + NORM_EPS) * g


def _memkv_kernel(mem_ref, g_ref, w_ref, k_ref, v_ref, kb_ref, vb_ref):
    d = k_ref.shape[1]
    kv = _dot(_rms(mem_ref[...], g_ref[...]).astype(BF16), w_ref[...])
    k, v = kv[:, :d], kv[:, d:]
    k_ref[...] = k
    v_ref[...] = v
    kb_ref[...] = k.astype(BF16)
    vb_ref[...] = v.astype(BF16)


def _memkv(mem2d, g, w_kv):
    n, d = mem2d.shape
    dm = w_kv.shape[1] // 2
    tm = TOKEN_BLOCK
    return pl.pallas_call(
        _memkv_kernel,
        grid=(n // tm,),
        in_specs=[_rows(tm, d), _resident((1, d)), _resident(w_kv.shape)],
        out_specs=[_rows(tm, dm)] * 4,
        out_shape=[jax.ShapeDtypeStruct((n, dm), F32)] * 2 + [jax.ShapeDtypeStruct((n, dm), BF16)] * 2,
        compiler_params=_params(1),
        name="memkv",
    )(mem2d, g, w_kv)


def _inproj_kernel(x_ref, cos_ref, sin_ref, gmix_ref, ggm_ref, gq_ref, gkv_ref,
                   wu_ref, wv_ref, wcq_ref, wckv_ref, wkr_ref, wqm_ref, wg_ref, wq3_ref, wuk_ref, wuv_ref,
                   u_ref, vn_ref, q_ref, ckv_ref, kr_ref, kn_ref, vm_ref, krp_ref, qm_ref, gate_ref,
                   *vn32_ref):
    xn = _rms(x_ref[...], gmix_ref[...]).astype(BF16)
    cos = cos_ref[...]
    sin = sin_ref[...]

    u_ref[...] = jax.nn.gelu(_dot(xn, wu_ref[...])).astype(BF16)
    vn = _rms(jax.nn.gelu(_dot(xn, wv_ref[...])), ggm_ref[...])
    vn_ref[...] = vn.astype(BF16)
    if vn32_ref:
        vn32_ref[0][...] = vn

    cqn = _rms(_dot(xn, wcq_ref[...]), gq_ref[...]).astype(BF16)
    q3 = _dot(cqn, wq3_ref[...])
    nh = MLA_HEADS * LANE
    for h in range(MLA_HEADS):
        lo = h * LANE
        q_ref[:, 2 * lo:2 * lo + LANE] = (q3[:, lo:lo + LANE] * MLA_QSCALE).astype(BF16)
        rot = q3[:, nh + lo:nh + lo + LANE] * cos + q3[:, 2 * nh + lo:2 * nh + lo + LANE] * sin
        q_ref[:, 2 * lo + LANE:2 * lo + 2 * LANE] = (rot * MLA_QSCALE).astype(BF16)

    ckv = _rms(_dot(xn, wckv_ref[...]), gkv_ref[...])
    ckv_ref[...] = ckv
    ckvb = ckv.astype(BF16)
    kn_ref[...] = _dot(ckvb, wuk_ref[...]).astype(BF16)
    vm_ref[...] = _dot(ckvb, wuv_ref[...]).astype(BF16)
    kr2 = _dot(xn, wkr_ref[...])
    krr = kr2[:, :LANE] * cos + kr2[:, LANE:] * sin
    kr_ref[...] = krr[:, :ROPE_DIM]
    krp_ref[...] = krr.astype(BF16)

    qm_ref[...] = (_dot(xn, wqm_ref[...]) * MEM_QSCALE).astype(BF16)
    gate_ref[...] = jax.nn.sigmoid(_dot(xn, wg_ref[...])).astype(BF16)


def _inproj(x2d, cos_tab, sin_tab, gains, weights, want_v32):
    n, d = x2d.shape
    tm = TOKEN_BLOCK
    tab_blocks = cos_tab.shape[0] // tm
    tab_spec = pl.BlockSpec((tm, LANE), lambda i: (i % tab_blocks, 0))
    out_widths = [(d, BF16), (d, BF16), (MLA_HEADS * HEAD_PAD, BF16), (weights[3].shape[1], F32), (ROPE_DIM, F32),
                  (MLA_HEADS * NOPE_DIM, BF16), (MLA_HEADS * V_DIM, BF16), (LANE, BF16),
                  (MEM_HEADS * MEM_HEAD_DIM, BF16), (3 * d, BF16)]
    if want_v32:
        out_widths.append((d, F32))
    return pl.pallas_call(
        _inproj_kernel,
        grid=(n // tm,),
        in_specs=([_rows(tm, d), tab_spec, tab_spec] + [_resident(g.shape) for g in gains]
                  + [_resident(w.shape) for w in weights]),
        out_specs=[_rows(tm, w) for w, _ in out_widths],
        out_shape=[jax.ShapeDtypeStruct((n, w), dt) for w, dt in out_widths],
        compiler_params=_params(1),
        name="inproj",
    )(x2d, cos_tab, sin_tab, *gains, *weights)


def _kvup_kernel(ckv_ref, kr_ref, wuk_ref, wuv_ref, kn_ref, vm_ref, krp_ref):
    ckvb = ckv_ref[...].astype(BF16)
    kn_ref[...] = _dot(ckvb, wuk_ref[...]).astype(BF16)
    vm_ref[...] = _dot(ckvb, wuv_ref[...]).astype(BF16)
    kr = kr_ref[...]
    krp_ref[...] = jnp.concatenate([kr, jnp.zeros_like(kr)], axis=1).astype(BF16)


def _kvup(ckv2d, kr2d, w_uk, w_uv):
    n, c = ckv2d.shape
    tm = TOKEN_BLOCK
    return pl.pallas_call(
        _kvup_kernel,
        grid=(n // tm,),
        in_specs=[_rows(tm, c), _rows(tm, ROPE_DIM), _resident(w_uk.shape), _resident(w_uv.shape)],
        out_specs=[_rows(tm, w_uk.shape[1]), _rows(tm, w_uv.shape[1]), _rows(tm, LANE)],
        out_shape=[jax.ShapeDtypeStruct((n, w_uk.shape[1]), BF16), jax.ShapeDtypeStruct((n, w_uv.shape[1]), BF16),
                   jax.ShapeDtypeStruct((n, LANE), BF16)],
        compiler_params=_params(1),
        name="kvup",
    )(ckv2d, kr2d, w_uk, w_uv)


def _tril_weights(gw_ref, g, rows):
    w = gw_ref[g][:rows, :rows]
    r = lax.broadcasted_iota(jnp.int32, (rows, rows), 0)
    c = lax.broadcasted_iota(jnp.int32, (rows, rows), 1)
    return jnp.where(c <= r, w, 0.0).astype(BF16)


def _mem_attend(qm_ref, mk_ref, mv_ref, o_ref, col0):
    for h in range(MEM_HEADS):
        sl = slice(h * MEM_HEAD_DIM, (h + 1) * MEM_HEAD_DIM)
        s = _dot_nt(qm_ref[:, sl], mk_ref[:, sl])
        p = jnp.exp2(s - jnp.max(s, axis=1, keepdims=True))
        o = _dot(p.astype(BF16), mv_ref[:, sl]) / jnp.sum(p, axis=1, keepdims=True)
        o_ref[:, col0 + h * MEM_HEAD_DIM:col0 + (h + 1) * MEM_HEAD_DIM] = o.astype(BF16)


def _mix_prompt_kernel(u_ref, vn_ref, q_ref, qm_ref, kn_ref, vm_ref, krp_ref, mk_ref, mv_ref, gw_ref, gb_ref, o_ref):
    qi = pl.program_id(1)
    tq, d = u_ref.shape

    for g in range(GM_GROUPS):
        wg = _tril_weights(gw_ref, g, GM_CHUNK)
        bias = gb_ref[g]
        cs = slice(g * LANE, (g + 1) * LANE)
        for r0 in range(0, tq, 2 * GM_CHUNK):
            ra = slice(r0, r0 + GM_CHUNK)
            rb = slice(r0 + GM_CHUNK, r0 + 2 * GM_CHUNK)
            mixed = _dot(wg, jnp.concatenate([vn_ref[ra, cs], vn_ref[rb, cs]], axis=1))
            o_ref[ra, cs] = (u_ref[ra, cs].astype(F32) * (mixed[:, :LANE] + bias)).astype(BF16)
            o_ref[rb, cs] = (u_ref[rb, cs].astype(F32) * (mixed[:, LANE:] + bias)).astype(BF16)

    r = lax.broadcasted_iota(jnp.int32, (tq, tq), 0) // CHUNK
    c = lax.broadcasted_iota(jnp.int32, (tq, tq), 1) // CHUNK
    visible = c <= r
    for h in range(MLA_HEADS):
        hs = slice(h * LANE, (h + 1) * LANE)
        qh = q_ref[:, h * HEAD_PAD:(h + 1) * HEAD_PAD]

        def scores(start, hs=hs, qh=qh):
            rows = pl.ds(start, tq)
            kh = jnp.concatenate([kn_ref[rows, hs], krp_ref[rows, :]], axis=1)
            return _dot_nt(qh, kh), vm_ref[rows, hs]

        s, vh = scores(pl.multiple_of(qi * tq, tq))
        s = jnp.where(visible, s, -jnp.inf)
        m = jnp.max(s, axis=1, keepdims=True)
        p = jnp.exp2(s - m)
        l = jnp.sum(p, axis=1, keepdims=True)
        acc = _dot(p.astype(BF16), vh)

        def past_block(j, carry, scores=scores):
            m, l, acc = carry
            s, vh = scores(pl.multiple_of(j * tq, tq))
            m_new = jnp.maximum(m, jnp.max(s, axis=1, keepdims=True))
            a = jnp.exp2(m - m_new)
            p = jnp.exp2(s - m_new)
            return m_new, a * l + jnp.sum(p, axis=1, keepdims=True), a * acc + _dot(p.astype(BF16), vh)

        m, l, acc = lax.fori_loop(0, qi, past_block, (m, l, acc))
        o_ref[:, d + h * V_DIM:d + (h + 1) * V_DIM] = (acc / l).astype(BF16)

    _mem_attend(qm_ref, mk_ref, mv_ref, o_ref, d + MLA_HEADS * V_DIM)


def _mix_prompt(batch, seq, u, vn, q, qm, kn, vm, krp, mkb, mvb, gw, gb):
    n, d = u.shape
    tq = Q_BLOCK
    nq = seq // tq
    n_mem = mkb.shape[0] // batch
    blk = lambda w: pl.BlockSpec((tq, w), lambda b, i: (b * nq + i, 0))
    per_batch = lambda rows, w: pl.BlockSpec((rows, w), lambda b, i: (b, 0))
    return pl.pallas_call(
        _mix_prompt_kernel,
        grid=(batch, nq),
        in_specs=[blk(d), blk(d), blk(q.shape[1]), blk(qm.shape[1]),
                  per_batch(seq, kn.shape[1]), per_batch(seq, vm.shape[1]), per_batch(seq, LANE),
                  per_batch(n_mem, mkb.shape[1]), per_batch(n_mem, mvb.shape[1]),
                  _resident(gw.shape), _resident(gb.shape)],
        out_specs=blk(3 * d),
        out_shape=jax.ShapeDtypeStruct((n, 3 * d), BF16),
        compiler_params=_params(2),
        name="mix_prompt",
    )(u, vn, q, qm, kn, vm, krp, mkb, mvb, gw, gb)


def _mix_sample_kernel(u_ref, vn_ref, q_ref, qm_ref, knp_ref, vmp_ref, krpp_ref, knn_ref, vmn_ref, krpn_ref,
                       mk_ref, mv_ref, gw_ref, gb_ref, o_ref):
    ts, d = u_ref.shape

    for g in range(GM_GROUPS):
        cs = slice(g * LANE, (g + 1) * LANE)
        mixed = _dot(_tril_weights(gw_ref, g, ts), vn_ref[:, cs]) + gb_ref[g][:ts, :]
        o_ref[:, cs] = (u_ref[:, cs].astype(F32) * mixed).astype(BF16)

    for h in range(MLA_HEADS):
        hs = slice(h * LANE, (h + 1) * LANE)
        qh = q_ref[:, h * HEAD_PAD:(h + 1) * HEAD_PAD]
        s_past = _dot_nt(qh, jnp.concatenate([knp_ref[:, hs], krpp_ref[...]], axis=1))
        s_new = _dot_nt(qh, jnp.concatenate([knn_ref[:, hs], krpn_ref[...]], axis=1))
        m = jnp.maximum(jnp.max(s_past, axis=1, keepdims=True), jnp.max(s_new, axis=1, keepdims=True))
        p_past = jnp.exp2(s_past - m)
        p_new = jnp.exp2(s_new - m)
        l = jnp.sum(p_past, axis=1, keepdims=True) + jnp.sum(p_new, axis=1, keepdims=True)
        acc = _dot(p_past.astype(BF16), vmp_ref[:, hs]) + _dot(p_new.astype(BF16), vmn_ref[:, hs])
        o_ref[:, d + h * V_DIM:d + (h + 1) * V_DIM] = (acc / l).astype(BF16)

    _mem_attend(qm_ref, mk_ref, mv_ref, o_ref, d + MLA_HEADS * V_DIM)


def _mix_sample(batch, ts, past, u, vn, q, qm, knp, vmp, krpp, knn, vmn, krpn, mkb, mvb, gw, gb):
    n, d = u.shape
    n_mem = mkb.shape[0] // batch
    per_batch = lambda rows, w: pl.BlockSpec((rows, w), lambda b: (b, 0))
    return pl.pallas_call(
        _mix_sample_kernel,
        grid=(batch,),
        in_specs=[per_batch(ts, d), per_batch(ts, d), per_batch(ts, q.shape[1]), per_batch(ts, qm.shape[1]),
                  per_batch(past, knp.shape[1]), per_batch(past, vmp.shape[1]), per_batch(past, LANE),
                  per_batch(ts, knn.shape[1]), per_batch(ts, vmn.shape[1]), per_batch(ts, LANE),
                  per_batch(n_mem, mkb.shape[1]), per_batch(n_mem, mvb.shape[1]),
                  _resident(gw.shape), _resident(gb.shape)],
        out_specs=per_batch(ts, 3 * d),
        out_shape=jax.ShapeDtypeStruct((n, 3 * d), BF16),
        compiler_params=_params(1),
        name="mix_sample",
    )(u, vn, q, qm, knp, vmp, krpp, knn, vmn, krpn, mkb, mvb, gw, gb)


def _merge_ffn_kernel(x_ref, o_ref, gate_ref, gffn_ref, gfin_ref, wgm_ref, wmla_ref, wmem_ref, wout_ref,
                      wfg_ref, wfu_ref, wfd_ref, y_ref):
    d = x_ref.shape[1]
    merged = (gate_ref[:, 0:d].astype(F32) * _dot(o_ref[:, 0:d], wgm_ref[...])
              + gate_ref[:, d:2 * d].astype(F32) * _dot(o_ref[:, d:2 * d], wmla_ref[...])
              + gate_ref[:, 2 * d:3 * d].astype(F32) * _dot(o_ref[:, 2 * d:3 * d], wmem_ref[...]))
    h = x_ref[...] + _dot(merged.astype(BF16), wout_ref[...])
    hn = _rms(h, gffn_ref[...]).astype(BF16)
    a = jax.nn.silu(_dot(hn, wfg_ref[...])) * _dot(hn, wfu_ref[...])
    h = h + _dot(a.astype(BF16), wfd_ref[...])
    y_ref[...] = _rms(h, gfin_ref[...])


def _merge_ffn(x2d, o_cat, gates, g_ffn, g_fin, weights):
    n, d = x2d.shape
    tm = TOKEN_BLOCK
    return pl.pallas_call(
        _merge_ffn_kernel,
        grid=(n // tm,),
        in_specs=([_rows(tm, d), _rows(tm, 3 * d), _rows(tm, 3 * d), _resident(g_ffn.shape), _resident(g_fin.shape)]
                  + [_resident(w.shape) for w in weights]),
        out_specs=_rows(tm, d),
        out_shape=jax.ShapeDtypeStruct((n, d), F32),
        compiler_params=_params(1),
        name="merge_ffn",
    )(x2d, o_cat, gates, g_ffn, g_fin, *weights)


def _rope_tables(pos):
    inv = ROPE_THETA ** (-jnp.arange(0, ROPE_DIM, 2, dtype=F32) / ROPE_DIM)
    ang = pos.astype(F32)[:, None] * inv[None, :]
    cos, sin = jnp.cos(ang), jnp.sin(ang)
    zero = jnp.zeros((pos.shape[0], LANE - ROPE_DIM), F32)
    return jnp.concatenate([cos, cos, zero], axis=1), jnp.concatenate([-sin, sin, zero], axis=1)


def _swap_halves(w):
    half = w.shape[-1] // 2
    return jnp.concatenate([w[..., half:], w[..., :half]], axis=-1)


def _pad_lanes(w):
    return jnp.pad(w, [(0, 0)] * (w.ndim - 1) + [(0, LANE - w.shape[-1])])


def kernel(x_prompt, x_sample, cache_mla_ckv, cache_mla_krope, cache_mem_k, cache_mem_v, mem_prompt, norm_mix_g, w_in, gm_norm_g, gm_ws, gm_bs, mla_q_norm_g, mla_w_uq, mla_kv_norm_g, mla_w_uk, mla_w_uv, mem_norm_g, mem_w_kv, w_br_gm, w_br_mla, w_br_mem, w_out, norm_ffn_g, ffn_w_gate, ffn_w_up, ffn_w_down, final_norm_g):
    depth = w_in.shape[0]
    assert depth == 1, "single-layer step"
    batch, seq, d = x_prompt.shape
    dec_batch, dec_seq, _ = x_sample.shape
    past = cache_mla_ckv.shape[2]
    n_mem = mem_prompt.shape[1]
    q_lora = mla_w_uq.shape[1]
    kv_lora = mla_w_uk.shape[1]
    d_mem = MEM_HEADS * MEM_HEAD_DIM
    assert seq % Q_BLOCK == 0 and (batch * seq) % TOKEN_BLOCK == 0 and past % GM_CHUNK == 0
    assert past % CHUNK == 0 and dec_seq <= CHUNK, "new tokens must sit in one attention chunk"

    w = w_in[0]
    o = 0
    parts = []
    for width in (d, d, q_lora, kv_lora, ROPE_DIM, d_mem, 3 * d):
        parts.append(w[:, o:o + width])
        o += width
    w_u, w_v, w_cq, w_ckv, w_kr, w_qm, w_gate = parts
    w_kr2 = jnp.concatenate([_pad_lanes(w_kr), _pad_lanes(_swap_halves(w_kr))], axis=1)
    uq = mla_w_uq[0]
    uq_rope = uq[:, :, NOPE_DIM:]
    w_q3 = jnp.concatenate([uq[:, :, :NOPE_DIM].reshape(q_lora, -1),
                            _pad_lanes(uq_rope).reshape(q_lora, -1),
                            _pad_lanes(_swap_halves(uq_rope)).reshape(q_lora, -1)], axis=1)
    w_uk = mla_w_uk[0].reshape(kv_lora, -1).astype(BF16)
    w_uv = mla_w_uv[0].reshape(kv_lora, -1).astype(BF16)
    in_weights = [x.astype(BF16) for x in (w_u, w_v, w_cq, w_ckv, w_kr2, w_qm, w_gate, w_q3)] + [w_uk, w_uv]
    in_gains = [norm_mix_g[0][None], gm_norm_g[0][None], mla_q_norm_g[0][None], mla_kv_norm_g[0][None]]
    out_weights = [x[0].astype(BF16) for x in (w_br_gm, w_br_mla, w_br_mem, w_out, ffn_w_gate, ffn_w_up, ffn_w_down)]
    gw = gm_ws[0]
    gb = jnp.broadcast_to(gm_bs[0][:, :, None], gm_bs.shape[1:] + (LANE,))

    mk, mv, mkb, mvb = _memkv(mem_prompt.reshape(batch * n_mem, d), mem_norm_g[0][None], mem_w_kv[0].astype(BF16))
    xp = x_prompt.reshape(batch * seq, d)
    cos_p, sin_p = _rope_tables(jnp.arange(seq, dtype=jnp.int32))
    u, vn, q, ckv_p, kr_p, kn, vm, krp, qm, gates = _inproj(xp, cos_p, sin_p, in_gains, in_weights, False)
    o_cat = _mix_prompt(batch, seq, u, vn, q, qm, kn, vm, krp, mkb, mvb, gw, gb)
    y_prompt = _merge_ffn(xp, o_cat, gates, norm_ffn_g[0][None], final_norm_g[None], out_weights)

    xs = x_sample.reshape(dec_batch * dec_seq, d)
    cos_s, sin_s = _rope_tables(past + jnp.arange(dec_seq, dtype=jnp.int32))
    reps = max(TOKEN_BLOCK // dec_seq, 1)
    cos_s, sin_s = jnp.tile(cos_s, (reps, 1)), jnp.tile(sin_s, (reps, 1))
    u, vn, q, ckv_s, kr_s, knn, vmn, krpn, qm, gates, gv_s = _inproj(xs, cos_s, sin_s, in_gains, in_weights, True)
    knp, vmp, krpp = _kvup(cache_mla_ckv[0].reshape(dec_batch * past, kv_lora),
                           cache_mla_krope[0].reshape(dec_batch * past, ROPE_DIM), w_uk, w_uv)
    cmk = cache_mem_k[0].reshape(dec_batch * n_mem, d_mem).astype(BF16)
    cmv = cache_mem_v[0].reshape(dec_batch * n_mem, d_mem).astype(BF16)
    o_cat = _mix_sample(dec_batch, dec_seq, past, u, vn, q, qm, knp, vmp, krpp, knn, vmn, krpn, cmk, cmv, gw, gb)
    y_sample = _merge_ffn(xs, o_cat, gates, norm_ffn_g[0][None], final_norm_g[None], out_weights)

    return (y_prompt.reshape(batch, seq, d), y_sample.reshape(dec_batch, dec_seq, d),
            ckv_p.reshape(1, batch, seq, kv_lora), kr_p.reshape(1, batch, seq, ROPE_DIM),
            mk.reshape(1, batch, n_mem, MEM_HEADS, MEM_HEAD_DIM), mv.reshape(1, batch, n_mem, MEM_HEADS, MEM_HEAD_DIM),
            ckv_s.reshape(1, dec_batch, dec_seq, kv_lora), kr_s.reshape(1, dec_batch, dec_seq, ROPE_DIM),
            gv_s.reshape(1, dec_batch, dec_seq, d))
```

```python
import functools
import math

import jax
import jax.numpy as jnp
from jax import lax
from jax.experimental import pallas as pl
from jax.experimental.pallas import tpu as pltpu

F32 = jnp.float32
BF16 = jnp.bfloat16

NORM_EPS = 1e-6
CHUNK = 64
GM_CHUNK = 128
GM_GROUPS = 8
MLA_HEADS = 8
NOPE_DIM = 128
ROPE_DIM = 64
V_DIM = 128
ROPE_THETA = 10000.0
MEM_HEADS = 4
MEM_HEAD_DIM = 256
LOG2E = math.log2(math.e)
MLA_QSCALE = (NOPE_DIM + ROPE_DIM) ** -0.5 * LOG2E
MEM_QSCALE = MEM_HEAD_DIM ** -0.5 * LOG2E

LANE = 128
HEAD_PAD = 2 * LANE
SUM_ROWS = 16
SOFTMAX_LAG = 3
VALUES_LAG = 6
VMEM_LIMIT_BYTES = 56 * 1024 * 1024

TOKEN_BLOCK = 256
Q_BLOCK = 256


def _params(n_axes):
    return pltpu.CompilerParams(dimension_semantics=("arbitrary",) * n_axes,
                                vmem_limit_bytes=VMEM_LIMIT_BYTES)


def _resident(shape):
    return pl.BlockSpec(shape, lambda *_: (0,) * len(shape), pipeline_mode=pl.Buffered(1))


def _rows(tm, width):
    return pl.BlockSpec((tm, width), lambda i: (i, 0))


def _dot(a, b):
    return jnp.dot(a, b, preferred_element_type=F32)


def _dot_nt(a, b):
    return lax.dot_general(a, b, (((1,), (1,)), ((), ())), preferred_element_type=F32)


def _rms(x, g):
    return x * lax.rsqrt(jnp.mean(x * x, axis=-1, keepdims=True) + NORM_EPS) * g


def _memkv_kernel(mem_ref, g_ref, w_ref, k_ref, v_ref, kb_ref, vb_ref):
    d = k_ref.shape[1]
    kv = _dot(_rms(mem_ref[...], g_ref[...]).astype(BF16), w_ref[...])
    k, v = kv[:, :d], kv[:, d:]
    k_ref[...] = k
    v_ref[...] = v
    kb_ref[...] = k.astype(BF16)
    vb_ref[...] = v.astype(BF16)


def _memkv(mem2d, g, w_kv):
    n, d = mem2d.shape
    dm = w_kv.shape[1] // 2
    tm = TOKEN_BLOCK
    return pl.pallas_call(
        _memkv_kernel,
        grid=(n // tm,),
        in_specs=[_rows(tm, d), _resident((1, d)), _resident(w_kv.shape)],
        out_specs=[_rows(tm, dm)] * 4,
        out_shape=[jax.ShapeDtypeStruct((n, dm), F32)] * 2 + [jax.ShapeDtypeStruct((n, dm), BF16)] * 2,
        compiler_params=_params(1),
        name="memkv",
    )(mem2d, g, w_kv)


def _inproj_kernel(x_ref, cos_ref, sin_ref, gmix_ref, ggm_ref, gq_ref, gkv_ref,
                   wu_ref, wv_ref, wcq_ref, wckv_ref, wkr_ref, wqm_ref, wg_ref, wq3_ref, wuk_ref, wuv_ref,
                   u_ref, vn_ref, q_ref, ckv_ref, kr_ref, kn_ref, vm_ref, krp_ref, qm_ref, gate_ref,
                   *vn32_ref, values_transposed):
    xn = _rms(x_ref[...], gmix_ref[...]).astype(BF16)
    cos = cos_ref[...]
    sin = sin_ref[...]

    u_ref[...] = jax.nn.gelu(_dot(xn, wu_ref[...])).astype(BF16)
    vn = _rms(jax.nn.gelu(_dot(xn, wv_ref[...])), ggm_ref[...])
    vn_ref[...] = vn.astype(BF16)
    if vn32_ref:
        vn32_ref[0][...] = vn

    cqn = _rms(_dot(xn, wcq_ref[...]), gq_ref[...]).astype(BF16)
    q3 = _dot(cqn, wq3_ref[...])
    nh = MLA_HEADS * LANE
    for h in range(MLA_HEADS):
        lo = h * LANE
        q_ref[:, 2 * lo:2 * lo + LANE] = (q3[:, lo:lo + LANE] * MLA_QSCALE).astype(BF16)
        rot = q3[:, nh + lo:nh + lo + LANE] * cos + q3[:, 2 * nh + lo:2 * nh + lo + LANE] * sin
        q_ref[:, 2 * lo + LANE:2 * lo + 2 * LANE] = (rot * MLA_QSCALE).astype(BF16)

    ckv = _rms(_dot(xn, wckv_ref[...]), gkv_ref[...])
    ckv_ref[...] = ckv
    ckvb = ckv.astype(BF16)
    kn_ref[...] = _dot(ckvb, wuk_ref[...]).astype(BF16)
    if values_transposed:
        for j in range(vm_ref.shape[0]):
            rows = slice(j * Q_BLOCK, (j + 1) * Q_BLOCK)
            vm_ref[j] = _dot_nt(wuv_ref[...], ckvb[rows, :]).astype(BF16)
    else:
        vm_ref[...] = _dot(ckvb, wuv_ref[...]).astype(BF16)
    kr2 = _dot(xn, wkr_ref[...])
    krr = kr2[:, :LANE] * cos + kr2[:, LANE:] * sin
    kr_ref[...] = krr[:, :ROPE_DIM]
    krp_ref[...] = krr.astype(BF16)

    qm_ref[...] = (_dot(xn, wqm_ref[...]) * MEM_QSCALE).astype(BF16)
    gate_ref[...] = jax.nn.sigmoid(_dot(xn, wg_ref[...])).astype(BF16)


def _inproj(x2d, cos_tab, sin_tab, gains, weights, want_v32, values_transposed):
    n, d = x2d.shape
    tm = TOKEN_BLOCK
    hv = MLA_HEADS * V_DIM
    tab_blocks = cos_tab.shape[0] // tm
    tab_spec = pl.BlockSpec((tm, LANE), lambda i: (i % tab_blocks, 0))
    out_widths = [(d, BF16), (d, BF16), (MLA_HEADS * HEAD_PAD, BF16), (weights[3].shape[1], F32), (ROPE_DIM, F32),
                  (MLA_HEADS * NOPE_DIM, BF16), (hv, BF16), (LANE, BF16),
                  (MEM_HEADS * MEM_HEAD_DIM, BF16), (3 * d, BF16)]
    if want_v32:
        out_widths.append((d, F32))
    out_specs = [_rows(tm, w) for w, _ in out_widths]
    out_shape = [jax.ShapeDtypeStruct((n, w), dt) for w, dt in out_widths]
    if values_transposed:
        kb = tm // Q_BLOCK
        out_specs[6] = pl.BlockSpec((kb, hv, Q_BLOCK), lambda i: (i, 0, 0))
        out_shape[6] = jax.ShapeDtypeStruct((n // Q_BLOCK, hv, Q_BLOCK), BF16)
    return pl.pallas_call(
        functools.partial(_inproj_kernel, values_transposed=values_transposed),
        grid=(n // tm,),
        in_specs=([_rows(tm, d), tab_spec, tab_spec] + [_resident(g.shape) for g in gains]
                  + [_resident(w.shape) for w in weights]),
        out_specs=out_specs,
        out_shape=out_shape,
        compiler_params=_params(1),
        name="inproj",
    )(x2d, cos_tab, sin_tab, *gains, *weights)


def _kvup_kernel(ckv_ref, kr_ref, wuk_ref, wuv_ref, kn_ref, vm_ref, krp_ref):
    ckvb = ckv_ref[...].astype(BF16)
    kn_ref[...] = _dot(ckvb, wuk_ref[...]).astype(BF16)
    vm_ref[...] = _dot(ckvb, wuv_ref[...]).astype(BF16)
    kr = kr_ref[...]
    krp_ref[...] = jnp.concatenate([kr, jnp.zeros_like(kr)], axis=1).astype(BF16)


def _kvup(ckv2d, kr2d, w_uk, w_uv):
    n, c = ckv2d.shape
    tm = TOKEN_BLOCK
    return pl.pallas_call(
        _kvup_kernel,
        grid=(n // tm,),
        in_specs=[_rows(tm, c), _rows(tm, ROPE_DIM), _resident(w_uk.shape), _resident(w_uv.shape)],
        out_specs=[_rows(tm, w_uk.shape[1]), _rows(tm, w_uv.shape[1]), _rows(tm, LANE)],
        out_shape=[jax.ShapeDtypeStruct((n, w_uk.shape[1]), BF16), jax.ShapeDtypeStruct((n, w_uv.shape[1]), BF16),
                   jax.ShapeDtypeStruct((n, LANE), BF16)],
        compiler_params=_params(1),
        name="kvup",
    )(ckv2d, kr2d, w_uk, w_uv)


def _tril_weights(gw_ref, g, rows):
    w = gw_ref[g][:rows, :rows]
    r = lax.broadcasted_iota(jnp.int32, (rows, rows), 0)
    c = lax.broadcasted_iota(jnp.int32, (rows, rows), 1)
    return jnp.where(c <= r, w, 0.0).astype(BF16)


def _mem_attend(qm_ref, mk_ref, mv_ref, o_ref, col0):
    for h in range(MEM_HEADS):
        sl = slice(h * MEM_HEAD_DIM, (h + 1) * MEM_HEAD_DIM)
        s = _dot_nt(qm_ref[:, sl], mk_ref[:, sl])
        p = jnp.exp2(s - jnp.max(s, axis=1, keepdims=True))
        o = _dot(p.astype(BF16), mv_ref[:, sl]) / jnp.sum(p, axis=1, keepdims=True)
        o_ref[:, col0 + h * MEM_HEAD_DIM:col0 + (h + 1) * MEM_HEAD_DIM] = o.astype(BF16)


def _mix_prompt_kernel(u_ref, vn_ref, q_ref, qm_ref, kn_ref, vt_ref, krp_ref, mk_ref, mv_ref, gw_ref, gb_ref, o_ref,
                       m_sc, acc_sc):
    qi = pl.program_id(1)
    tq, d = u_ref.shape

    for g in range(GM_GROUPS):
        wg = _tril_weights(gw_ref, g, GM_CHUNK)
        bias = gb_ref[g]
        cs = slice(g * LANE, (g + 1) * LANE)
        for r0 in range(0, tq, 2 * GM_CHUNK):
            ra = slice(r0, r0 + GM_CHUNK)
            rb = slice(r0 + GM_CHUNK, r0 + 2 * GM_CHUNK)
            mixed = _dot(wg, jnp.concatenate([vn_ref[ra, cs], vn_ref[rb, cs]], axis=1))
            o_ref[ra, cs] = (u_ref[ra, cs].astype(F32) * (mixed[:, :LANE] + bias)).astype(BF16)
            o_ref[rb, cs] = (u_ref[rb, cs].astype(F32) * (mixed[:, LANE:] + bias)).astype(BF16)

    kc = lax.broadcasted_iota(jnp.int32, (tq, tq), 0) // CHUNK
    qc = lax.broadcasted_iota(jnp.int32, (tq, tq), 1) // CHUNK
    visible = kc <= qc

    ones_rows = jnp.ones((SUM_ROWS, tq), BF16)

    def key_block(j, first):
        rows = pl.ds(pl.multiple_of(j * tq, tq), tq)
        krp = krp_ref[rows, :]
        scores, probs = {}, {}

        def stage_scores(h):
            hs = slice(h * LANE, (h + 1) * LANE)
            kh = jnp.concatenate([kn_ref[rows, hs], krp], axis=1)
            scores[h] = _dot_nt(kh, q_ref[:, h * HEAD_PAD:(h + 1) * HEAD_PAD])

        def stage_softmax(h):
            st = scores.pop(h)
            if first:
                st = jnp.where(visible, st, -jnp.inf)
                m = jnp.max(st, axis=0, keepdims=True)
                a = None
            else:
                m_old = m_sc[h]
                m = jnp.maximum(m_old, jnp.max(st, axis=0, keepdims=True))
                a = jnp.exp2(m_old - m)
            m_sc[h] = m
            probs[h] = (jnp.exp2(st - m).astype(BF16), a)

        def stage_values(h):
            p, a = probs.pop(h)
            vt = jnp.concatenate([vt_ref[j, h * LANE:(h + 1) * LANE, :], ones_rows], axis=0)
            pv = _dot(vt, p)
            acc_sc[h] = pv if first else a * acc_sc[h] + pv

        for step in range(MLA_HEADS + VALUES_LAG):
            if step < MLA_HEADS:
                stage_scores(step)
            if 0 <= step - SOFTMAX_LAG < MLA_HEADS:
                stage_softmax(step - SOFTMAX_LAG)
            if 0 <= step - VALUES_LAG < MLA_HEADS:
                stage_values(step - VALUES_LAG)

    key_block(qi, True)

    def past_block(j, carry):
        key_block(j, False)
        return carry

    lax.fori_loop(0, qi, past_block, 0)
    for h in range(MLA_HEADS):
        acc = acc_sc[h]
        o = acc[:V_DIM, :] * (1.0 / acc[V_DIM:V_DIM + 1, :])
        o_ref[:, d + h * V_DIM:d + (h + 1) * V_DIM] = o.T.astype(BF16)

    _mem_attend(qm_ref, mk_ref, mv_ref, o_ref, d + MLA_HEADS * V_DIM)


def _mix_prompt(batch, seq, u, vn, q, qm, kn, vt, krp, mkb, mvb, gw, gb):
    n, d = u.shape
    tq = Q_BLOCK
    nq = seq // tq
    n_mem = mkb.shape[0] // batch
    blk = lambda w: pl.BlockSpec((tq, w), lambda b, i: (b * nq + i, 0))
    per_batch = lambda rows, w: pl.BlockSpec((rows, w), lambda b, i: (b, 0))
    return pl.pallas_call(
        _mix_prompt_kernel,
        grid=(batch, nq),
        in_specs=[blk(d), blk(d), blk(q.shape[1]), blk(qm.shape[1]),
                  per_batch(seq, kn.shape[1]),
                  pl.BlockSpec((nq,) + vt.shape[1:], lambda b, i: (b, 0, 0)),
                  per_batch(seq, LANE),
                  per_batch(n_mem, mkb.shape[1]), per_batch(n_mem, mvb.shape[1]),
                  _resident(gw.shape), _resident(gb.shape)],
        out_specs=blk(3 * d),
        out_shape=jax.ShapeDtypeStruct((n, 3 * d), BF16),
        scratch_shapes=[pltpu.VMEM((MLA_HEADS, 1, tq), F32), pltpu.VMEM((MLA_HEADS, V_DIM + SUM_ROWS, tq), F32)],
        compiler_params=_params(2),
        name="mix_prompt",
    )(u, vn, q, qm, kn, vt, krp, mkb, mvb, gw, gb)


def _mix_sample_kernel(u_ref, vn_ref, q_ref, qm_ref, knp_ref, vmp_ref, krpp_ref, knn_ref, vmn_ref, krpn_ref,
                       mk_ref, mv_ref, gw_ref, gb_ref, o_ref):
    ts, d = u_ref.shape

    for g in range(GM_GROUPS):
        cs = slice(g * LANE, (g + 1) * LANE)
        mixed = _dot(_tril_weights(gw_ref, g, ts), vn_ref[:, cs]) + gb_ref[g][:ts, :]
        o_ref[:, cs] = (u_ref[:, cs].astype(F32) * mixed).astype(BF16)

    for h in range(MLA_HEADS):
        hs = slice(h * LANE, (h + 1) * LANE)
        qh = q_ref[:, h * HEAD_PAD:(h + 1) * HEAD_PAD]
        s_past = _dot_nt(qh, jnp.concatenate([knp_ref[:, hs], krpp_ref[...]], axis=1))
        s_new = _dot_nt(qh, jnp.concatenate([knn_ref[:, hs], krpn_ref[...]], axis=1))
        m = jnp.maximum(jnp.max(s_past, axis=1, keepdims=True), jnp.max(s_new, axis=1, keepdims=True))
        p_past = jnp.exp2(s_past - m)
        p_new = jnp.exp2(s_new - m)
        l = jnp.sum(p_past, axis=1, keepdims=True) + jnp.sum(p_new, axis=1, keepdims=True)
        acc = _dot(p_past.astype(BF16), vmp_ref[:, hs]) + _dot(p_new.astype(BF16), vmn_ref[:, hs])
        o_ref[:, d + h * V_DIM:d + (h + 1) * V_DIM] = (acc / l).astype(BF16)

    _mem_attend(qm_ref, mk_ref, mv_ref, o_ref, d + MLA_HEADS * V_DIM)


def _mix_sample(batch, ts, past, u, vn, q, qm, knp, vmp, krpp, knn, vmn, krpn, mkb, mvb, gw, gb):
    n, d = u.shape
    n_mem = mkb.shape[0] // batch
    per_batch = lambda rows, w: pl.BlockSpec((rows, w), lambda b: (b, 0))
    return pl.pallas_call(
        _mix_sample_kernel,
        grid=(batch,),
        in_specs=[per_batch(ts, d), per_batch(ts, d), per_batch(ts, q.shape[1]), per_batch(ts, qm.shape[1]),
                  per_batch(past, knp.shape[1]), per_batch(past, vmp.shape[1]), per_batch(past, LANE),
                  per_batch(ts, knn.shape[1]), per_batch(ts, vmn.shape[1]), per_batch(ts, LANE),
                  per_batch(n_mem, mkb.shape[1]), per_batch(n_mem, mvb.shape[1]),
                  _resident(gw.shape), _resident(gb.shape)],
        out_specs=per_batch(ts, 3 * d),
        out_shape=jax.ShapeDtypeStruct((n, 3 * d), BF16),
        compiler_params=_params(1),
        name="mix_sample",
    )(u, vn, q, qm, knp, vmp, krpp, knn, vmn, krpn, mkb, mvb, gw, gb)


def _merge_ffn_kernel(x_ref, o_ref, gate_ref, gffn_ref, gfin_ref, wgm_ref, wmla_ref, wmem_ref, wout_ref,
                      wfg_ref, wfu_ref, wfd_ref, y_ref):
    d = x_ref.shape[1]
    merged = (gate_ref[:, 0:d].astype(F32) * _dot(o_ref[:, 0:d], wgm_ref[...])
              + gate_ref[:, d:2 * d].astype(F32) * _dot(o_ref[:, d:2 * d], wmla_ref[...])
              + gate_ref[:, 2 * d:3 * d].astype(F32) * _dot(o_ref[:, 2 * d:3 * d], wmem_ref[...]))
    h = x_ref[...] + _dot(merged.astype(BF16), wout_ref[...])
    hn = _rms(h, gffn_ref[...]).astype(BF16)
    a = jax.nn.silu(_dot(hn, wfg_ref[...])) * _dot(hn, wfu_ref[...])
    h = h + _dot(a.astype(BF16), wfd_ref[...])
    y_ref[...] = _rms(h, gfin_ref[...])


def _merge_ffn(x2d, o_cat, gates, g_ffn, g_fin, weights):
    n, d = x2d.shape
    tm = TOKEN_BLOCK
    return pl.pallas_call(
        _merge_ffn_kernel,
        grid=(n // tm,),
        in_specs=([_rows(tm, d), _rows(tm, 3 * d), _rows(tm, 3 * d), _resident(g_ffn.shape), _resident(g_fin.shape)]
                  + [_resident(w.shape) for w in weights]),
        out_specs=_rows(tm, d),
        out_shape=jax.ShapeDtypeStruct((n, d), F32),
        compiler_params=_params(1),
        name="merge_ffn",
    )(x2d, o_cat, gates, g_ffn, g_fin, *weights)


def _rope_tables(pos):
    inv = ROPE_THETA ** (-jnp.arange(0, ROPE_DIM, 2, dtype=F32) / ROPE_DIM)
    ang = pos.astype(F32)[:, None] * inv[None, :]
    cos, sin = jnp.cos(ang), jnp.sin(ang)
    zero = jnp.zeros((pos.shape[0], LANE - ROPE_DIM), F32)
    return jnp.concatenate([cos, cos, zero], axis=1), jnp.concatenate([-sin, sin, zero], axis=1)


def _swap_halves(w):
    half = w.shape[-1] // 2
    return jnp.concatenate([w[..., half:], w[..., :half]], axis=-1)


def _pad_lanes(w):
    return jnp.pad(w, [(0, 0)] * (w.ndim - 1) + [(0, LANE - w.shape[-1])])


def kernel(x_prompt, x_sample, cache_mla_ckv, cache_mla_krope, cache_mem_k, cache_mem_v, mem_prompt, norm_mix_g, w_in, gm_norm_g, gm_ws, gm_bs, mla_q_norm_g, mla_w_uq, mla_kv_norm_g, mla_w_uk, mla_w_uv, mem_norm_g, mem_w_kv, w_br_gm, w_br_mla, w_br_mem, w_out, norm_ffn_g, ffn_w_gate, ffn_w_up, ffn_w_down, final_norm_g):
    depth = w_in.shape[0]
    assert depth == 1, "single-layer step"
    batch, seq, d = x_prompt.shape
    dec_batch, dec_seq, _ = x_sample.shape
    past = cache_mla_ckv.shape[2]
    n_mem = mem_prompt.shape[1]
    q_lora = mla_w_uq.shape[1]
    kv_lora = mla_w_uk.shape[1]
    d_mem = MEM_HEADS * MEM_HEAD_DIM
    assert seq % Q_BLOCK == 0 and (batch * seq) % TOKEN_BLOCK == 0 and past % GM_CHUNK == 0
    assert past % CHUNK == 0 and dec_seq <= CHUNK, "new tokens must sit in one attention chunk"

    w = w_in[0]
    o = 0
    parts = []
    for width in (d, d, q_lora, kv_lora, ROPE_DIM, d_mem, 3 * d):
        parts.append(w[:, o:o + width])
        o += width
    w_u, w_v, w_cq, w_ckv, w_kr, w_qm, w_gate = parts
    w_kr2 = jnp.concatenate([_pad_lanes(w_kr), _pad_lanes(_swap_halves(w_kr))], axis=1)
    uq = mla_w_uq[0]
    uq_rope = uq[:, :, NOPE_DIM:]
    w_q3 = jnp.concatenate([uq[:, :, :NOPE_DIM].reshape(q_lora, -1),
                            _pad_lanes(uq_rope).reshape(q_lora, -1),
                            _pad_lanes(_swap_halves(uq_rope)).reshape(q_lora, -1)], axis=1)
    w_uk = mla_w_uk[0].reshape(kv_lora, -1).astype(BF16)
    w_uv = mla_w_uv[0].reshape(kv_lora, -1).astype(BF16)
    in_weights = [x.astype(BF16) for x in (w_u, w_v, w_cq, w_ckv, w_kr2, w_qm, w_gate, w_q3)] + [w_uk]
    in_gains = [norm_mix_g[0][None], gm_norm_g[0][None], mla_q_norm_g[0][None], mla_kv_norm_g[0][None]]
    out_weights = [x[0].astype(BF16) for x in (w_br_gm, w_br_mla, w_br_mem, w_out, ffn_w_gate, ffn_w_up, ffn_w_down)]
    gw = gm_ws[0]
    gb = jnp.broadcast_to(gm_bs[0][:, :, None], gm_bs.shape[1:] + (LANE,))

    mk, mv, mkb, mvb = _memkv(mem_prompt.reshape(batch * n_mem, d), mem_norm_g[0][None], mem_w_kv[0].astype(BF16))
    xp = x_prompt.reshape(batch * seq, d)
    cos_p, sin_p = _rope_tables(jnp.arange(seq, dtype=jnp.int32))
    u, vn, q, ckv_p, kr_p, kn, vt, krp, qm, gates = _inproj(xp, cos_p, sin_p, in_gains, in_weights + [w_uv.T],
                                                            False, True)
    o_cat = _mix_prompt(batch, seq, u, vn, q, qm, kn, vt, krp, mkb, mvb, gw, gb)
    y_prompt = _merge_ffn(xp, o_cat, gates, norm_ffn_g[0][None], final_norm_g[None], out_weights)

    xs = x_sample.reshape(dec_batch * dec_seq, d)
    cos_s, sin_s = _rope_tables(past + jnp.arange(dec_seq, dtype=jnp.int32))
    reps = max(TOKEN_BLOCK // dec_seq, 1)
    cos_s, sin_s = jnp.tile(cos_s, (reps, 1)), jnp.tile(sin_s, (reps, 1))
    u, vn, q, ckv_s, kr_s, knn, vmn, krpn, qm, gates, gv_s = _inproj(xs, cos_s, sin_s, in_gains, in_weights + [w_uv],
                                                                     True, False)
    knp, vmp, krpp = _kvup(cache_mla_ckv[0].reshape(dec_batch * past, kv_lora),
                           cache_mla_krope[0].reshape(dec_batch * past, ROPE_DIM), w_uk, w_uv)
    cmk = cache_mem_k[0].reshape(dec_batch * n_mem, d_mem).astype(BF16)
    cmv = cache_mem_v[0].reshape(dec_batch * n_mem, d_mem).astype(BF16)
    o_cat = _mix_sample(dec_batch, dec_seq, past, u, vn, q, qm, knp, vmp, krpp, knn, vmn, krpn, cmk, cmv, gw, gb)
    y_sample = _merge_ffn(xs, o_cat, gates, norm_ffn_g[0][None], final_norm_g[None], out_weights)

    return (y_prompt.reshape(batch, seq, d), y_sample.reshape(dec_batch, dec_seq, d),
            ckv_p.reshape(1, batch, seq, kv_lora), kr_p.reshape(1, batch, seq, ROPE_DIM),
            mk.reshape(1, batch, n_mem, MEM_HEADS, MEM_HEAD_DIM), mv.reshape(1, batch, n_mem, MEM_HEADS, MEM_HEAD_DIM),
            ckv_s.reshape(1, dec_batch, dec_seq, kv_lora), kr_s.reshape(1, dec_batch, dec_seq, ROPE_DIM),
            gv_s.reshape(1, dec_batch, dec_seq, d))
```

```python
import functools
import math

import jax
import jax.numpy as jnp
from jax import lax
from jax.experimental import pallas as pl
from jax.experimental.pallas import tpu as pltpu

F32 = jnp.float32
BF16 = jnp.bfloat16

NORM_EPS = 1e-6
CHUNK = 64
GM_CHUNK = 128
GM_GROUPS = 8
MLA_HEADS = 8
NOPE_DIM = 128
ROPE_DIM = 64
V_DIM = 128
ROPE_THETA = 10000.0
MEM_HEADS = 4
MEM_HEAD_DIM = 256
LOG2E = math.log2(math.e)
MLA_QSCALE = (NOPE_DIM + ROPE_DIM) ** -0.5 * LOG2E
MEM_QSCALE = MEM_HEAD_DIM ** -0.5 * LOG2E

LANE = 128
HEAD_PAD = 2 * LANE
SUM_ROWS = 16
SOFTMAX_LAG = 1
VALUES_LAG = 2
VMEM_LIMIT_BYTES = 56 * 1024 * 1024

TOKEN_BLOCK = 256
INPROJ_BLOCK = 512
MERGE_BLOCK = 512
Q_BLOCK = 512
KEY_SLAB = 256


def _params(n_axes):
    return pltpu.CompilerParams(dimension_semantics=("arbitrary",) * n_axes,
                                vmem_limit_bytes=VMEM_LIMIT_BYTES)


def _resident(shape):
    return pl.BlockSpec(shape, lambda *_: (0,) * len(shape), pipeline_mode=pl.Buffered(1))


def _rows(tm, width):
    return pl.BlockSpec((tm, width), lambda i: (i, 0))


def _dot(a, b):
    return jnp.dot(a, b, preferred_element_type=F32)


def _dot_nt(a, b):
    return lax.dot_general(a, b, (((1,), (1,)), ((), ())), preferred_element_type=F32)


def _rms(x, g):
    return x * lax.rsqrt(jnp.mean(x * x, axis=-1, keepdims=True) + NORM_EPS) * g


def _memkv_kernel(mem_ref, g_ref, w_ref, k_ref, v_ref, kb_ref, vb_ref):
    d = k_ref.shape[1]
    kv = _dot(_rms(mem_ref[...], g_ref[...]).astype(BF16), w_ref[...])
    k, v = kv[:, :d], kv[:, d:]
    k_ref[...] = k
    v_ref[...] = v
    kb_ref[...] = k.astype(BF16)
    vb_ref[...] = v.astype(BF16)


def _memkv(mem2d, g, w_kv):
    n, d = mem2d.shape
    dm = w_kv.shape[1] // 2
    tm = TOKEN_BLOCK
    return pl.pallas_call(
        _memkv_kernel,
        grid=(n // tm,),
        in_specs=[_rows(tm, d), _resident((1, d)), _resident(w_kv.shape)],
        out_specs=[_rows(tm, dm)] * 4,
        out_shape=[jax.ShapeDtypeStruct((n, dm), F32)] * 2 + [jax.ShapeDtypeStruct((n, dm), BF16)] * 2,
        compiler_params=_params(1),
        name="memkv",
    )(mem2d, g, w_kv)


def _inproj_kernel(x_ref, cos_ref, sin_ref, gmix_ref, ggm_ref, gq_ref, gkv_ref,
                   wu_ref, wv_ref, wcq_ref, wckv_ref, wkr_ref, wqm_ref, wg_ref, wq3_ref, *rest, prompt):
    if prompt:
        wuk_ref, wuvt_ref, u_ref, vn_ref, q_ref, ckv_ref, kr_ref, qm_ref, gate_ref, kn_ref, vt_ref, krp_ref = rest
    else:
        u_ref, vn_ref, q_ref, ckv_ref, kr_ref, qm_ref, gate_ref, vn32_ref = rest
    xn = _rms(x_ref[...], gmix_ref[...]).astype(BF16)
    cos = cos_ref[...]
    sin = sin_ref[...]

    u_ref[...] = jax.nn.gelu(_dot(xn, wu_ref[...])).astype(BF16)
    vn = _rms(jax.nn.gelu(_dot(xn, wv_ref[...])), ggm_ref[...])
    vn_ref[...] = vn.astype(BF16)
    if not prompt:
        vn32_ref[...] = vn

    cqn = _rms(_dot(xn, wcq_ref[...]), gq_ref[...]).astype(BF16)
    q3 = _dot(cqn, wq3_ref[...])
    nn, nr = MLA_HEADS * NOPE_DIM, MLA_HEADS * ROPE_DIM
    zeros = jnp.zeros((q3.shape[0], LANE - ROPE_DIM), F32)
    for pair in range(MLA_HEADS // 2):
        lo = pair * LANE
        rot = (q3[:, nn + lo:nn + lo + LANE] * cos + q3[:, nn + nr + lo:nn + nr + lo + LANE] * sin) * MLA_QSCALE
        for h, piece in ((2 * pair, rot[:, :ROPE_DIM]), (2 * pair + 1, rot[:, ROPE_DIM:])):
            q_ref[:, h * HEAD_PAD:h * HEAD_PAD + LANE] = (q3[:, h * LANE:(h + 1) * LANE] * MLA_QSCALE).astype(BF16)
            q_ref[:, h * HEAD_PAD + LANE:(h + 1) * HEAD_PAD] = jnp.concatenate([piece, zeros], axis=1).astype(BF16)

    ckv = _rms(_dot(xn, wckv_ref[...]), gkv_ref[...])
    ckv_ref[...] = ckv
    kr2 = _dot(xn, wkr_ref[...])
    krr = kr2[:, :LANE] * cos + kr2[:, LANE:] * sin
    kr_ref[...] = krr[:, :ROPE_DIM]
    if prompt:
        ckvb = ckv.astype(BF16)
        kn_ref[...] = _dot(ckvb, wuk_ref[...]).astype(BF16)
        for j in range(vt_ref.shape[0]):
            vt_ref[j] = _dot_nt(wuvt_ref[...], ckvb[j * KEY_SLAB:(j + 1) * KEY_SLAB, :]).astype(BF16)
        krp_ref[...] = krr.astype(BF16)

    qm_ref[...] = (_dot(xn, wqm_ref[...]) * MEM_QSCALE).astype(BF16)
    gate_ref[...] = jax.nn.sigmoid(_dot(xn, wg_ref[...])).astype(BF16)


def _inproj(x2d, cos_tab, sin_tab, gains, weights, kv_weights=None):
    n, d = x2d.shape
    tm = INPROJ_BLOCK
    prompt = kv_weights is not None
    tab_blocks = cos_tab.shape[0] // tm
    tab_spec = pl.BlockSpec((tm, LANE), lambda i: (i % tab_blocks, 0))
    outs = [(d, BF16), (d, BF16), (MLA_HEADS * HEAD_PAD, BF16), (weights[3].shape[1], F32), (ROPE_DIM, F32),
            (MEM_HEADS * MEM_HEAD_DIM, BF16), (3 * d, BF16)]
    outs += [(MLA_HEADS * NOPE_DIM, BF16), None, (LANE, BF16)] if prompt else [(d, F32)]
    out_specs = [w and _rows(tm, w[0]) for w in outs]
    out_shape = [w and jax.ShapeDtypeStruct((n, w[0]), w[1]) for w in outs]
    if prompt:
        hv = MLA_HEADS * V_DIM
        out_specs[8] = pl.BlockSpec((tm // KEY_SLAB, hv, KEY_SLAB), lambda i: (i, 0, 0))
        out_shape[8] = jax.ShapeDtypeStruct((n // KEY_SLAB, hv, KEY_SLAB), BF16)
        weights = list(weights) + list(kv_weights)
    return pl.pallas_call(
        functools.partial(_inproj_kernel, prompt=prompt),
        grid=(n // tm,),
        in_specs=([_rows(tm, d), tab_spec, tab_spec] + [_resident(g.shape) for g in gains]
                  + [_resident(w.shape) for w in weights]),
        out_specs=out_specs,
        out_shape=out_shape,
        compiler_params=_params(1),
        name="inproj",
    )(x2d, cos_tab, sin_tab, *gains, *weights)


def _tril_weights(gw_ref, g, rows):
    w = gw_ref[g][:rows, :rows]
    r = lax.broadcasted_iota(jnp.int32, (rows, rows), 0)
    c = lax.broadcasted_iota(jnp.int32, (rows, rows), 1)
    return jnp.where(c <= r, w, 0.0).astype(BF16)


def _mem_attend(qm_ref, mk_ref, mv_ref, o_ref, col0):
    for h in range(MEM_HEADS):
        sl = slice(h * MEM_HEAD_DIM, (h + 1) * MEM_HEAD_DIM)
        s = _dot_nt(qm_ref[:, sl], mk_ref[:, sl])
        p = jnp.exp2(s - jnp.max(s, axis=1, keepdims=True))
        o = _dot(p.astype(BF16), mv_ref[:, sl]) / jnp.sum(p, axis=1, keepdims=True)
        o_ref[:, col0 + h * MEM_HEAD_DIM:col0 + (h + 1) * MEM_HEAD_DIM] = o.astype(BF16)


def _mix_prompt_kernel(u_ref, vn_ref, q_ref, qm_ref, kn_ref, vt_ref, krp_ref, mk_ref, mv_ref, gw_ref, gb_ref, o_ref,
                       m_sc, acc_sc):
    qi = pl.program_id(1)
    tq, d = u_ref.shape

    for g in range(GM_GROUPS):
        wg = _tril_weights(gw_ref, g, GM_CHUNK)
        bias = gb_ref[g]
        cs = slice(g * LANE, (g + 1) * LANE)
        for r0 in range(0, tq, 2 * GM_CHUNK):
            ra = slice(r0, r0 + GM_CHUNK)
            rb = slice(r0 + GM_CHUNK, r0 + 2 * GM_CHUNK)
            mixed = _dot(wg, jnp.concatenate([vn_ref[ra, cs], vn_ref[rb, cs]], axis=1))
            o_ref[ra, cs] = (u_ref[ra, cs].astype(F32) * (mixed[:, :LANE] + bias)).astype(BF16)
            o_ref[rb, cs] = (u_ref[rb, cs].astype(F32) * (mixed[:, LANE:] + bias)).astype(BF16)

    kc = lax.broadcasted_iota(jnp.int32, (tq, tq), 0) // CHUNK
    qc = lax.broadcasted_iota(jnp.int32, (tq, tq), 1) // CHUNK
    visible = kc <= qc

    ones_rows = jnp.ones((SUM_ROWS, tq), BF16)

    def key_block(j, first):
        rows = pl.ds(pl.multiple_of(j * tq, tq), tq)
        krp = krp_ref[rows, :]
        scores, probs = {}, {}

        def stage_scores(h):
            hs = slice(h * LANE, (h + 1) * LANE)
            kh = jnp.concatenate([kn_ref[rows, hs], krp], axis=1)
            scores[h] = _dot_nt(kh, q_ref[:, h * HEAD_PAD:(h + 1) * HEAD_PAD])

        def stage_softmax(h):
            st = scores.pop(h)
            if first:
                st = jnp.where(visible, st, -jnp.inf)
                m = jnp.max(st, axis=0, keepdims=True)
                a = None
            else:
                m_old = m_sc[h]
                m = jnp.maximum(m_old, jnp.max(st, axis=0, keepdims=True))
                a = jnp.exp2(m_old - m)
            m_sc[h] = m
            probs[h] = (jnp.exp2(st - m).astype(BF16), a)

        def stage_values(h):
            p, a = probs.pop(h)
            slabs = tq // KEY_SLAB
            vt = jnp.concatenate([vt_ref[j * slabs + i, h * V_DIM:(h + 1) * V_DIM, :] for i in range(slabs)], axis=1)
            vt = jnp.concatenate([vt, ones_rows], axis=0)
            pv = _dot(vt, p)
            acc_sc[h] = pv if first else a * acc_sc[h] + pv

        for step in range(MLA_HEADS + VALUES_LAG):
            if step < MLA_HEADS:
                stage_scores(step)
            if 0 <= step - SOFTMAX_LAG < MLA_HEADS:
                stage_softmax(step - SOFTMAX_LAG)
            if 0 <= step - VALUES_LAG < MLA_HEADS:
                stage_values(step - VALUES_LAG)

    key_block(qi, True)

    def past_block(j, carry):
        key_block(j, False)
        return carry

    lax.fori_loop(0, qi, past_block, 0)
    for h in range(MLA_HEADS):
        acc = acc_sc[h]
        o = acc[:V_DIM, :] * (1.0 / acc[V_DIM:V_DIM + 1, :])
        o_ref[:, d + h * V_DIM:d + (h + 1) * V_DIM] = o.T.astype(BF16)

    _mem_attend(qm_ref, mk_ref, mv_ref, o_ref, d + MLA_HEADS * V_DIM)


def _mix_prompt(batch, seq, u, vn, q, qm, kn, vt, krp, mkb, mvb, gw, gb):
    n, d = u.shape
    tq = Q_BLOCK
    nq = seq // tq
    n_mem = mkb.shape[0] // batch
    blk = lambda w: pl.BlockSpec((tq, w), lambda b, i: (b * nq + i, 0))
    per_batch = lambda rows, w: pl.BlockSpec((rows, w), lambda b, i: (b, 0))
    return pl.pallas_call(
        _mix_prompt_kernel,
        grid=(batch, nq),
        in_specs=[blk(d), blk(d), blk(q.shape[1]), blk(qm.shape[1]),
                  per_batch(seq, kn.shape[1]),
                  pl.BlockSpec((seq // KEY_SLAB,) + vt.shape[1:], lambda b, i: (b, 0, 0)),
                  per_batch(seq, LANE),
                  per_batch(n_mem, mkb.shape[1]), per_batch(n_mem, mvb.shape[1]),
                  _resident(gw.shape), _resident(gb.shape)],
        out_specs=blk(3 * d),
        out_shape=jax.ShapeDtypeStruct((n, 3 * d), BF16),
        scratch_shapes=[pltpu.VMEM((MLA_HEADS, 1, tq), F32), pltpu.VMEM((MLA_HEADS, V_DIM + SUM_ROWS, tq), F32)],
        compiler_params=_params(2),
        name="mix_prompt",
    )(u, vn, q, qm, kn, vt, krp, mkb, mvb, gw, gb)


def _mix_sample_kernel(u_ref, vn_ref, q_ref, qm_ref, ckvp_ref, krp_ref, ckvn_ref, krn_ref, wuk_ref, wuv_ref,
                       mk_ref, mv_ref, gw_ref, gb_ref, o_ref):
    ts, d = u_ref.shape

    for g in range(GM_GROUPS):
        cs = slice(g * LANE, (g + 1) * LANE)
        mixed = _dot(_tril_weights(gw_ref, g, ts), vn_ref[:, cs]) + gb_ref[g][:ts, :]
        o_ref[:, cs] = (u_ref[:, cs].astype(F32) * mixed).astype(BF16)

    def keys(c_ref, r_ref):
        r = r_ref[...]
        c = c_ref[...].astype(BF16)
        return c, jnp.concatenate([c, jnp.concatenate([r, jnp.zeros_like(r)], axis=1).astype(BF16)], axis=1)

    cp, kp = keys(ckvp_ref, krp_ref)
    cn, kn = keys(ckvn_ref, krn_ref)
    qs = []
    for h in range(MLA_HEADS):
        q_abs = _dot_nt(q_ref[:, h * HEAD_PAD:h * HEAD_PAD + NOPE_DIM], wuk_ref[:, h * NOPE_DIM:(h + 1) * NOPE_DIM])
        qs.append(jnp.concatenate([q_abs.astype(BF16), q_ref[:, h * HEAD_PAD + NOPE_DIM:(h + 1) * HEAD_PAD]], axis=1))
    qa = jnp.concatenate(qs, axis=0)
    s_past = _dot_nt(qa, kp)
    s_new = _dot_nt(qa, kn)
    m = jnp.maximum(jnp.max(s_past, axis=1, keepdims=True), jnp.max(s_new, axis=1, keepdims=True))
    p_past = jnp.exp2(s_past - m)
    p_new = jnp.exp2(s_new - m)
    l = jnp.sum(p_past, axis=1, keepdims=True) + jnp.sum(p_new, axis=1, keepdims=True)
    lat = ((_dot(p_past.astype(BF16), cp) + _dot(p_new.astype(BF16), cn)) / l).astype(BF16)
    for h in range(MLA_HEADS):
        o_ref[:, d + h * V_DIM:d + (h + 1) * V_DIM] = _dot(lat[h * ts:(h + 1) * ts, :],
                                                          wuv_ref[:, h * V_DIM:(h + 1) * V_DIM]).astype(BF16)

    _mem_attend(qm_ref, mk_ref, mv_ref, o_ref, d + MLA_HEADS * V_DIM)


def _mix_sample(batch, ts, past, u, vn, q, qm, ckv_past, kr_past, ckv_new, kr_new, w_uk, w_uv, mkb, mvb, gw, gb):
    n, d = u.shape
    n_mem = mkb.shape[0] // batch
    per_batch = lambda rows, w: pl.BlockSpec((rows, w), lambda b: (b, 0))
    return pl.pallas_call(
        _mix_sample_kernel,
        grid=(batch,),
        in_specs=[per_batch(ts, d), per_batch(ts, d), per_batch(ts, q.shape[1]), per_batch(ts, qm.shape[1]),
                  per_batch(past, ckv_past.shape[1]), per_batch(past, ROPE_DIM),
                  per_batch(ts, ckv_new.shape[1]), per_batch(ts, ROPE_DIM),
                  _resident(w_uk.shape), _resident(w_uv.shape),
                  per_batch(n_mem, mkb.shape[1]), per_batch(n_mem, mvb.shape[1]),
                  _resident(gw.shape), _resident(gb.shape)],
        out_specs=per_batch(ts, 3 * d),
        out_shape=jax.ShapeDtypeStruct((n, 3 * d), BF16),
        compiler_params=_params(1),
        name="mix_sample",
    )(u, vn, q, qm, ckv_past, kr_past, ckv_new, kr_new, w_uk, w_uv, mkb, mvb, gw, gb)


def _merge_ffn_kernel(x_ref, o_ref, gate_ref, gffn_ref, gfin_ref, wgm_ref, wmla_ref, wmem_ref, wout_ref,
                      wfg_ref, wfu_ref, wfd_ref, y_ref):
    d = x_ref.shape[1]
    merged = (gate_ref[:, 0:d].astype(F32) * _dot(o_ref[:, 0:d], wgm_ref[...])
              + gate_ref[:, d:2 * d].astype(F32) * _dot(o_ref[:, d:2 * d], wmla_ref[...])
              + gate_ref[:, 2 * d:3 * d].astype(F32) * _dot(o_ref[:, 2 * d:3 * d], wmem_ref[...]))
    h = x_ref[...] + _dot(merged.astype(BF16), wout_ref[...])
    hn = _rms(h, gffn_ref[...]).astype(BF16)
    a = jax.nn.silu(_dot(hn, wfg_ref[...])) * _dot(hn, wfu_ref[...])
    h = h + _dot(a.astype(BF16), wfd_ref[...])
    y_ref[...] = _rms(h, gfin_ref[...])


def _merge_ffn(x2d, o_cat, gates, g_ffn, g_fin, weights):
    n, d = x2d.shape
    tm = MERGE_BLOCK
    return pl.pallas_call(
        _merge_ffn_kernel,
        grid=(n // tm,),
        in_specs=([_rows(tm, d), _rows(tm, 3 * d), _rows(tm, 3 * d), _resident(g_ffn.shape), _resident(g_fin.shape)]
                  + [_resident(w.shape) for w in weights]),
        out_specs=_rows(tm, d),
        out_shape=jax.ShapeDtypeStruct((n, d), F32),
        compiler_params=_params(1),
        name="merge_ffn",
    )(x2d, o_cat, gates, g_ffn, g_fin, *weights)


def _rope_tables(pos):
    inv = ROPE_THETA ** (-jnp.arange(0, ROPE_DIM, 2, dtype=F32) / ROPE_DIM)
    ang = pos.astype(F32)[:, None] * inv[None, :]
    cos, sin = jnp.cos(ang), jnp.sin(ang)
    return jnp.concatenate([cos, cos, cos, cos], axis=1), jnp.concatenate([-sin, sin, -sin, sin], axis=1)


def _swap_halves(w):
    half = w.shape[-1] // 2
    return jnp.concatenate([w[..., half:], w[..., :half]], axis=-1)


def _pad_lanes(w):
    return jnp.pad(w, [(0, 0)] * (w.ndim - 1) + [(0, LANE - w.shape[-1])])


def kernel(x_prompt, x_sample, cache_mla_ckv, cache_mla_krope, cache_mem_k, cache_mem_v, mem_prompt, norm_mix_g, w_in, gm_norm_g, gm_ws, gm_bs, mla_q_norm_g, mla_w_uq, mla_kv_norm_g, mla_w_uk, mla_w_uv, mem_norm_g, mem_w_kv, w_br_gm, w_br_mla, w_br_mem, w_out, norm_ffn_g, ffn_w_gate, ffn_w_up, ffn_w_down, final_norm_g):
    depth = w_in.shape[0]
    assert depth == 1, "single-layer step"
    batch, seq, d = x_prompt.shape
    dec_batch, dec_seq, _ = x_sample.shape
    past = cache_mla_ckv.shape[2]
    n_mem = mem_prompt.shape[1]
    q_lora = mla_w_uq.shape[1]
    kv_lora = mla_w_uk.shape[1]
    d_mem = MEM_HEADS * MEM_HEAD_DIM
    assert seq % Q_BLOCK == 0 and seq % INPROJ_BLOCK == 0 and (batch * seq) % MERGE_BLOCK == 0
    assert (dec_batch * dec_seq) % INPROJ_BLOCK == 0 and (dec_batch * dec_seq) % MERGE_BLOCK == 0
    assert INPROJ_BLOCK % dec_seq == 0 and past % GM_CHUNK == 0
    assert past % CHUNK == 0 and dec_seq <= CHUNK, "new tokens must sit in one attention chunk"

    w = w_in[0]
    o = 0
    parts = []
    for width in (d, d, q_lora, kv_lora, ROPE_DIM, d_mem, 3 * d):
        parts.append(w[:, o:o + width])
        o += width
    w_u, w_v, w_cq, w_ckv, w_kr, w_qm, w_gate = parts
    w_kr2 = jnp.concatenate([_pad_lanes(w_kr), _pad_lanes(_swap_halves(w_kr))], axis=1)
    uq = mla_w_uq[0]
    uq_rope = uq[:, :, NOPE_DIM:]
    w_q3 = jnp.concatenate([uq[:, :, :NOPE_DIM].reshape(q_lora, -1),
                            uq_rope.reshape(q_lora, -1),
                            _swap_halves(uq_rope).reshape(q_lora, -1)], axis=1)
    w_uk = mla_w_uk[0].reshape(kv_lora, -1).astype(BF16)
    w_uv = mla_w_uv[0].reshape(kv_lora, -1).astype(BF16)
    in_weights = [x.astype(BF16) for x in (w_u, w_v, w_cq, w_ckv, w_kr2, w_qm, w_gate, w_q3)]
    in_gains = [norm_mix_g[0][None], gm_norm_g[0][None], mla_q_norm_g[0][None], mla_kv_norm_g[0][None]]
    out_weights = [x[0].astype(BF16) for x in (w_br_gm, w_br_mla, w_br_mem, w_out, ffn_w_gate, ffn_w_up, ffn_w_down)]
    gw = gm_ws[0]
    gb = jnp.broadcast_to(gm_bs[0][:, :, None], gm_bs.shape[1:] + (LANE,))

    mk, mv, mkb, mvb = _memkv(mem_prompt.reshape(batch * n_mem, d), mem_norm_g[0][None], mem_w_kv[0].astype(BF16))
    xp = x_prompt.reshape(batch * seq, d)
    cos_p, sin_p = _rope_tables(jnp.arange(seq, dtype=jnp.int32))
    u, vn, q, ckv_p, kr_p, qm, gates, kn, vt, krp = _inproj(xp, cos_p, sin_p, in_gains, in_weights, (w_uk, w_uv.T))
    o_cat = _mix_prompt(batch, seq, u, vn, q, qm, kn, vt, krp, mkb, mvb, gw, gb)
    y_prompt = _merge_ffn(xp, o_cat, gates, norm_ffn_g[0][None], final_norm_g[None], out_weights)

    xs = x_sample.reshape(dec_batch * dec_seq, d)
    cos_s, sin_s = _rope_tables(past + jnp.arange(dec_seq, dtype=jnp.int32))
    reps = max(INPROJ_BLOCK // dec_seq, 1)
    cos_s, sin_s = jnp.tile(cos_s, (reps, 1)), jnp.tile(sin_s, (reps, 1))
    u, vn, q, ckv_s, kr_s, qm, gates, gv_s = _inproj(xs, cos_s, sin_s, in_gains, in_weights)
    cmk = cache_mem_k[0].reshape(dec_batch * n_mem, d_mem).astype(BF16)
    cmv = cache_mem_v[0].reshape(dec_batch * n_mem, d_mem).astype(BF16)
    o_cat = _mix_sample(dec_batch, dec_seq, past, u, vn, q, qm,
                        cache_mla_ckv[0].reshape(dec_batch * past, kv_lora),
                        cache_mla_krope[0].reshape(dec_batch * past, ROPE_DIM), ckv_s, kr_s, w_uk, w_uv,
                        cmk, cmv, gw, gb)
    y_sample = _merge_ffn(xs, o_cat, gates, norm_ffn_g[0][None], final_norm_g[None], out_weights)

    return (y_prompt.reshape(batch, seq, d), y_sample.reshape(dec_batch, dec_seq, d),
            ckv_p.reshape(1, batch, seq, kv_lora), kr_p.reshape(1, batch, seq, ROPE_DIM),
            mk.reshape(1, batch, n_mem, MEM_HEADS, MEM_HEAD_DIM), mv.reshape(1, batch, n_mem, MEM_HEADS, MEM_HEAD_DIM),
            ckv_s.reshape(1, dec_batch, dec_seq, kv_lora), kr_s.reshape(1, dec_batch, dec_seq, ROPE_DIM),
            gv_s.reshape(1, dec_batch, dec_seq, d))
```

```python
import functools
import math

import jax
import jax.numpy as jnp
from jax import lax
from jax.experimental import pallas as pl
from jax.experimental.pallas import tpu as pltpu

F32 = jnp.float32
BF16 = jnp.bfloat16

NORM_EPS = 1e-6
CHUNK = 64
GM_CHUNK = 128
GM_GROUPS = 8
MLA_HEADS = 8
NOPE_DIM = 128
ROPE_DIM = 64
V_DIM = 128
ROPE_THETA = 10000.0
MEM_HEADS = 4
MEM_HEAD_DIM = 256
LOG2E = math.log2(math.e)
MLA_QSCALE = (NOPE_DIM + ROPE_DIM) ** -0.5 * LOG2E
MEM_QSCALE = MEM_HEAD_DIM ** -0.5 * LOG2E

LANE = 128
HEAD_PAD = 2 * LANE
SUM_ROWS = 16
SOFTMAX_LAG = 1
VALUES_LAG = 2
VMEM_LIMIT_BYTES = 56 * 1024 * 1024

TOKEN_BLOCK = 256
INPROJ_BLOCK = 512
MERGE_BLOCK = 512
Q_BLOCK = 512
KEY_SLAB = 256


def _params(n_axes):
    return pltpu.CompilerParams(dimension_semantics=("arbitrary",) * n_axes,
                                vmem_limit_bytes=VMEM_LIMIT_BYTES)


def _resident(shape):
    return pl.BlockSpec(shape, lambda *_: (0,) * len(shape), pipeline_mode=pl.Buffered(1))


def _rows(tm, width):
    return pl.BlockSpec((tm, width), lambda i: (i, 0))


def _dot(a, b):
    return jnp.dot(a, b, preferred_element_type=F32)


def _dot_nt(a, b):
    return lax.dot_general(a, b, (((1,), (1,)), ((), ())), preferred_element_type=F32)


def _rms(x, g):
    return x * lax.rsqrt(jnp.mean(x * x, axis=-1, keepdims=True) + NORM_EPS) * g


def _memkv_kernel(mem_ref, g_ref, w_ref, k_ref, v_ref, kb_ref, vb_ref):
    d = k_ref.shape[1]
    kv = _dot(_rms(mem_ref[...], g_ref[...]).astype(BF16), w_ref[...])
    k, v = kv[:, :d], kv[:, d:]
    k_ref[...] = k
    v_ref[...] = v
    kb_ref[...] = k.astype(BF16)
    vb_ref[...] = v.astype(BF16)


def _memkv(mem2d, g, w_kv):
    n, d = mem2d.shape
    dm = w_kv.shape[1] // 2
    tm = TOKEN_BLOCK
    return pl.pallas_call(
        _memkv_kernel,
        grid=(n // tm,),
        in_specs=[_rows(tm, d), _resident((1, d)), _resident(w_kv.shape)],
        out_specs=[_rows(tm, dm)] * 4,
        out_shape=[jax.ShapeDtypeStruct((n, dm), F32)] * 2 + [jax.ShapeDtypeStruct((n, dm), BF16)] * 2,
        compiler_params=_params(1),
        name="memkv",
    )(mem2d, g, w_kv)


def _inproj_kernel(x_ref, cos_ref, sin_ref, gmix_ref, ggm_ref, gq_ref, gkv_ref,
                   wu_ref, wv_ref, wc_ref, wckv_ref, wqm_ref, wg_ref, wq3_ref, *rest, prompt):
    if prompt:
        wuk_ref, wuvt_ref, u_ref, vn_ref, q_ref, ckv_ref, kr_ref, qm_ref, gate_ref, kn_ref, vt_ref, krp_ref = rest
    else:
        u_ref, vn_ref, q_ref, ckv_ref, kr_ref, qm_ref, gate_ref, vn32_ref = rest
    xn = _rms(x_ref[...], gmix_ref[...]).astype(BF16)
    cos = cos_ref[...]
    sin = sin_ref[...]
    d = x_ref.shape[1]
    q_lora = wq3_ref.shape[0]

    def gate_chunk(c):
        cols = slice(c * d, (c + 1) * d)
        gate_ref[:, cols] = jax.nn.sigmoid(_dot(xn, wg_ref[:, cols])).astype(BF16)

    zc = _dot(xn, wc_ref[...])
    zkv = _dot(xn, wckv_ref[...])
    zu = _dot(xn, wu_ref[...])

    cqn = _rms(zc[:, :q_lora], gq_ref[...]).astype(BF16)
    ckv = _rms(zkv, gkv_ref[...])
    ckv_ref[...] = ckv
    krk = zc[:, q_lora:]
    low = lax.broadcasted_iota(jnp.int32, (1, LANE), 1) < ROPE_DIM
    krr = krk * jnp.where(low, cos, 0.0) + pltpu.roll(krk, ROPE_DIM, 1) * jnp.where(low, sin, 0.0)
    kr_ref[...] = krr[:, :ROPE_DIM]

    q3 = _dot(cqn, wq3_ref[...])
    gate_chunk(0)

    u_ref[...] = jax.nn.gelu(zu).astype(BF16)
    zv = _dot(xn, wv_ref[...])
    if prompt:
        ckvb = ckv.astype(BF16)
        kn_ref[...] = _dot(ckvb, wuk_ref[...]).astype(BF16)
        for j in range(vt_ref.shape[0]):
            vt_ref[j] = _dot_nt(wuvt_ref[...], ckvb[j * KEY_SLAB:(j + 1) * KEY_SLAB, :]).astype(BF16)
        krp_ref[...] = krr.astype(BF16)

    nn, nr = MLA_HEADS * NOPE_DIM, MLA_HEADS * ROPE_DIM
    zeros = jnp.zeros((q3.shape[0], LANE - ROPE_DIM), F32)
    for pair in range(MLA_HEADS // 2):
        lo = pair * LANE
        rot = (q3[:, nn + lo:nn + lo + LANE] * cos + q3[:, nn + nr + lo:nn + nr + lo + LANE] * sin) * MLA_QSCALE
        for h, piece in ((2 * pair, rot[:, :ROPE_DIM]), (2 * pair + 1, rot[:, ROPE_DIM:])):
            q_ref[:, h * HEAD_PAD:h * HEAD_PAD + LANE] = (q3[:, h * LANE:(h + 1) * LANE] * MLA_QSCALE).astype(BF16)
            q_ref[:, h * HEAD_PAD + LANE:(h + 1) * HEAD_PAD] = jnp.concatenate([piece, zeros], axis=1).astype(BF16)
    gate_chunk(1)

    vn = _rms(jax.nn.gelu(zv), ggm_ref[...])
    vn_ref[...] = vn.astype(BF16)
    if not prompt:
        vn32_ref[...] = vn

    qm_ref[...] = (_dot(xn, wqm_ref[...]) * MEM_QSCALE).astype(BF16)
    gate_chunk(2)


def _inproj(x2d, cos_tab, sin_tab, gains, weights, kv_weights=None):
    n, d = x2d.shape
    tm = INPROJ_BLOCK
    prompt = kv_weights is not None
    tab_blocks = cos_tab.shape[0] // tm
    tab_spec = pl.BlockSpec((tm, LANE), lambda i: (i % tab_blocks, 0))
    outs = [(d, BF16), (d, BF16), (MLA_HEADS * HEAD_PAD, BF16), (weights[3].shape[1], F32), (ROPE_DIM, F32),
            (MEM_HEADS * MEM_HEAD_DIM, BF16), (3 * d, BF16)]
    outs += [(MLA_HEADS * NOPE_DIM, BF16), None, (LANE, BF16)] if prompt else [(d, F32)]
    out_specs = [w and _rows(tm, w[0]) for w in outs]
    out_shape = [w and jax.ShapeDtypeStruct((n, w[0]), w[1]) for w in outs]
    if prompt:
        hv = MLA_HEADS * V_DIM
        out_specs[8] = pl.BlockSpec((tm // KEY_SLAB, hv, KEY_SLAB), lambda i: (i, 0, 0))
        out_shape[8] = jax.ShapeDtypeStruct((n // KEY_SLAB, hv, KEY_SLAB), BF16)
        weights = list(weights) + list(kv_weights)
    return pl.pallas_call(
        functools.partial(_inproj_kernel, prompt=prompt),
        grid=(n // tm,),
        in_specs=([_rows(tm, d), tab_spec, tab_spec] + [_resident(g.shape) for g in gains]
                  + [_resident(w.shape) for w in weights]),
        out_specs=out_specs,
        out_shape=out_shape,
        compiler_params=_params(1),
        name="inproj",
    )(x2d, cos_tab, sin_tab, *gains, *weights)


def _tril_weights(gw_ref, g, rows):
    w = gw_ref[g][:rows, :rows]
    r = lax.broadcasted_iota(jnp.int32, (rows, rows), 0)
    c = lax.broadcasted_iota(jnp.int32, (rows, rows), 1)
    return jnp.where(c <= r, w, 0.0).astype(BF16)


def _mem_attend(qm_ref, mk_ref, mv_ref, o_ref, col0):
    def head(ref, h):
        if len(ref.shape) == 3:
            return ref[:, h, :].astype(BF16)
        return ref[:, h * MEM_HEAD_DIM:(h + 1) * MEM_HEAD_DIM]

    for h in range(MEM_HEADS):
        s = _dot_nt(qm_ref[:, h * MEM_HEAD_DIM:(h + 1) * MEM_HEAD_DIM], head(mk_ref, h))
        p = jnp.exp2(s - jnp.max(s, axis=1, keepdims=True))
        o = _dot(p.astype(BF16), head(mv_ref, h)) / jnp.sum(p, axis=1, keepdims=True)
        o_ref[:, col0 + h * MEM_HEAD_DIM:col0 + (h + 1) * MEM_HEAD_DIM] = o.astype(BF16)


def _mix_prompt_kernel(u_ref, vn_ref, q_ref, qm_ref, kn_ref, vt_ref, krp_ref, mk_ref, mv_ref, gw_ref, gb_ref, o_ref,
                       m_sc, acc_sc):
    qi = pl.program_id(1)
    tq, d = u_ref.shape

    for g in range(GM_GROUPS):
        wg = _tril_weights(gw_ref, g, GM_CHUNK)
        bias = gb_ref[g]
        cs = slice(g * LANE, (g + 1) * LANE)
        for r0 in range(0, tq, 2 * GM_CHUNK):
            ra = slice(r0, r0 + GM_CHUNK)
            rb = slice(r0 + GM_CHUNK, r0 + 2 * GM_CHUNK)
            mixed = _dot(wg, jnp.concatenate([vn_ref[ra, cs], vn_ref[rb, cs]], axis=1))
            o_ref[ra, cs] = (u_ref[ra, cs].astype(F32) * (mixed[:, :LANE] + bias)).astype(BF16)
            o_ref[rb, cs] = (u_ref[rb, cs].astype(F32) * (mixed[:, LANE:] + bias)).astype(BF16)

    half = tq // 2
    kc = lax.broadcasted_iota(jnp.int32, (half, half), 0) // CHUNK
    qc = lax.broadcasted_iota(jnp.int32, (half, half), 1) // CHUNK
    visible = kc <= qc

    ones_rows = jnp.ones((SUM_ROWS, tq), BF16)

    def key_block(j, first):
        rows = pl.ds(pl.multiple_of(j * tq, tq), tq)
        krp = krp_ref[rows, :]
        scores, probs = {}, {}

        def stage_scores(h):
            hs = slice(h * LANE, (h + 1) * LANE)
            kh = jnp.concatenate([kn_ref[rows, hs], krp], axis=1)
            qh = q_ref[:, h * HEAD_PAD:(h + 1) * HEAD_PAD]
            if first:
                scores[h] = (_dot_nt(kh[:half], qh[:half]), _dot_nt(kh, qh[half:]))
            else:
                scores[h] = _dot_nt(kh, qh)

        def stage_softmax(h):
            st = scores.pop(h)
            if first:
                s0, s1 = st
                s0 = jnp.where(visible, s0, -jnp.inf)
                s1 = jnp.concatenate([s1[:half], jnp.where(visible, s1[half:], -jnp.inf)], axis=0)
                m0 = jnp.max(s0, axis=0, keepdims=True)
                m1 = jnp.max(s1, axis=0, keepdims=True)
                m_sc[h] = jnp.concatenate([m0, m1], axis=1)
                probs[h] = (jnp.exp2(s0 - m0).astype(BF16), jnp.exp2(s1 - m1).astype(BF16))
            else:
                m_old = m_sc[h]
                m = jnp.maximum(m_old, jnp.max(st, axis=0, keepdims=True))
                m_sc[h] = m
                probs[h] = (jnp.exp2(st - m).astype(BF16), jnp.exp2(m_old - m))

        def stage_values(h):
            slabs = tq // KEY_SLAB
            vt = jnp.concatenate([vt_ref[j * slabs + i, h * V_DIM:(h + 1) * V_DIM, :] for i in range(slabs)], axis=1)
            vt = jnp.concatenate([vt, ones_rows], axis=0)
            if first:
                p0, p1 = probs.pop(h)
                acc_sc[h] = jnp.concatenate([_dot(vt[:, :half], p0), _dot(vt, p1)], axis=1)
            else:
                p, a = probs.pop(h)
                acc_sc[h] = a * acc_sc[h] + _dot(vt, p)

        for step in range(MLA_HEADS + VALUES_LAG):
            if step < MLA_HEADS:
                stage_scores(step)
            if 0 <= step - SOFTMAX_LAG < MLA_HEADS:
                stage_softmax(step - SOFTMAX_LAG)
            if 0 <= step - VALUES_LAG < MLA_HEADS:
                stage_values(step - VALUES_LAG)

    key_block(qi, True)

    def past_block(j, carry):
        key_block(j, False)
        return carry

    lax.fori_loop(0, qi, past_block, 0)
    for h in range(MLA_HEADS):
        acc = acc_sc[h]
        o = acc[:V_DIM, :] * (1.0 / acc[V_DIM:V_DIM + 1, :])
        o_ref[:, d + h * V_DIM:d + (h + 1) * V_DIM] = o.T.astype(BF16)

    _mem_attend(qm_ref, mk_ref, mv_ref, o_ref, d + MLA_HEADS * V_DIM)


def _mix_prompt(batch, seq, u, vn, q, qm, kn, vt, krp, mkb, mvb, gw, gb):
    n, d = u.shape
    tq = Q_BLOCK
    nq = seq // tq
    n_mem = mkb.shape[0] // batch
    blk = lambda w: pl.BlockSpec((tq, w), lambda b, i: (b * nq + i, 0))
    per_batch = lambda rows, w: pl.BlockSpec((rows, w), lambda b, i: (b, 0))
    return pl.pallas_call(
        _mix_prompt_kernel,
        grid=(batch, nq),
        in_specs=[blk(d), blk(d), blk(q.shape[1]), blk(qm.shape[1]),
                  per_batch(seq, kn.shape[1]),
                  pl.BlockSpec((seq // KEY_SLAB,) + vt.shape[1:], lambda b, i: (b, 0, 0)),
                  per_batch(seq, LANE),
                  per_batch(n_mem, mkb.shape[1]), per_batch(n_mem, mvb.shape[1]),
                  _resident(gw.shape), _resident(gb.shape)],
        out_specs=blk(3 * d),
        out_shape=jax.ShapeDtypeStruct((n, 3 * d), BF16),
        scratch_shapes=[pltpu.VMEM((MLA_HEADS, 1, tq), F32), pltpu.VMEM((MLA_HEADS, V_DIM + SUM_ROWS, tq), F32)],
        compiler_params=_params(2),
        name="mix_prompt",
    )(u, vn, q, qm, kn, vt, krp, mkb, mvb, gw, gb)


def _mix_sample_kernel(u_ref, vn_ref, q_ref, qm_ref, ckvp_ref, krp_ref, ckvn_ref, krn_ref, wuk_ref, wuv_ref,
                       mk_ref, mv_ref, gw_ref, gb_ref, o_ref):
    ts, d = u_ref.shape

    for g in range(GM_GROUPS):
        cs = slice(g * LANE, (g + 1) * LANE)
        mixed = _dot(_tril_weights(gw_ref, g, ts), vn_ref[:, cs]) + gb_ref[g][:ts, :]
        o_ref[:, cs] = (u_ref[:, cs].astype(F32) * mixed).astype(BF16)

    def keys(c_ref, r_ref):
        r = r_ref[...]
        c = c_ref[...].astype(BF16)
        return c, jnp.concatenate([c, jnp.concatenate([r, jnp.zeros_like(r)], axis=1).astype(BF16)], axis=1)

    cp, kp = keys(ckvp_ref, krp_ref)
    cn, kn = keys(ckvn_ref, krn_ref)
    qs = []
    for h in range(MLA_HEADS):
        q_abs = _dot_nt(q_ref[:, h * HEAD_PAD:h * HEAD_PAD + NOPE_DIM], wuk_ref[:, h * NOPE_DIM:(h + 1) * NOPE_DIM])
        qs.append(jnp.concatenate([q_abs.astype(BF16), q_ref[:, h * HEAD_PAD + NOPE_DIM:(h + 1) * HEAD_PAD]], axis=1))
    qa = jnp.concatenate(qs, axis=0)
    s_past = _dot_nt(qa, kp)
    s_new = _dot_nt(qa, kn)
    m = jnp.maximum(jnp.max(s_past, axis=1, keepdims=True), jnp.max(s_new, axis=1, keepdims=True))
    p_past = jnp.exp2(s_past - m)
    p_new = jnp.exp2(s_new - m)
    l = jnp.sum(p_past, axis=1, keepdims=True) + jnp.sum(p_new, axis=1, keepdims=True)
    lat = ((_dot(p_past.astype(BF16), cp) + _dot(p_new.astype(BF16), cn)) / l).astype(BF16)
    for h in range(MLA_HEADS):
        o_ref[:, d + h * V_DIM:d + (h + 1) * V_DIM] = _dot(lat[h * ts:(h + 1) * ts, :],
                                                          wuv_ref[:, h * V_DIM:(h + 1) * V_DIM]).astype(BF16)

    _mem_attend(qm_ref, mk_ref, mv_ref, o_ref, d + MLA_HEADS * V_DIM)


def _mix_sample(batch, ts, past, u, vn, q, qm, ckv_past, kr_past, ckv_new, kr_new, w_uk, w_uv, mem_k, mem_v, gw, gb):
    n, d = u.shape
    per_batch = lambda rows, w: pl.BlockSpec((rows, w), lambda b: (b, 0))
    mem_spec = pl.BlockSpec((None,) + mem_k.shape[1:], lambda b: (b, 0, 0, 0))
    return pl.pallas_call(
        _mix_sample_kernel,
        grid=(batch,),
        in_specs=[per_batch(ts, d), per_batch(ts, d), per_batch(ts, q.shape[1]), per_batch(ts, qm.shape[1]),
                  per_batch(past, ckv_past.shape[1]), per_batch(past, ROPE_DIM),
                  per_batch(ts, ckv_new.shape[1]), per_batch(ts, ROPE_DIM),
                  _resident(w_uk.shape), _resident(w_uv.shape), mem_spec, mem_spec,
                  _resident(gw.shape), _resident(gb.shape)],
        out_specs=per_batch(ts, 3 * d),
        out_shape=jax.ShapeDtypeStruct((n, 3 * d), BF16),
        compiler_params=_params(1),
        name="mix_sample",
    )(u, vn, q, qm, ckv_past, kr_past, ckv_new, kr_new, w_uk, w_uv, mem_k, mem_v, gw, gb)


def _merge_ffn_kernel(x_ref, o_ref, gate_ref, gffn_ref, gfin_ref, wgm_ref, wmla_ref, wmem_ref, wout_ref,
                      wfg_ref, wfu_ref, wfd_ref, y_ref):
    d = x_ref.shape[1]
    merged = (gate_ref[:, 0:d].astype(F32) * _dot(o_ref[:, 0:d], wgm_ref[...])
              + gate_ref[:, d:2 * d].astype(F32) * _dot(o_ref[:, d:2 * d], wmla_ref[...])
              + gate_ref[:, 2 * d:3 * d].astype(F32) * _dot(o_ref[:, 2 * d:3 * d], wmem_ref[...]))
    h = x_ref[...] + _dot(merged.astype(BF16), wout_ref[...])
    hn = _rms(h, gffn_ref[...]).astype(BF16)
    a = jax.nn.silu(_dot(hn, wfg_ref[...])) * _dot(hn, wfu_ref[...])
    h = h + _dot(a.astype(BF16), wfd_ref[...])
    y_ref[...] = _rms(h, gfin_ref[...])


def _merge_ffn(x2d, o_cat, gates, g_ffn, g_fin, weights):
    n, d = x2d.shape
    tm = MERGE_BLOCK
    return pl.pallas_call(
        _merge_ffn_kernel,
        grid=(n // tm,),
        in_specs=([_rows(tm, d), _rows(tm, 3 * d), _rows(tm, 3 * d), _resident(g_ffn.shape), _resident(g_fin.shape)]
                  + [_resident(w.shape) for w in weights]),
        out_specs=_rows(tm, d),
        out_shape=jax.ShapeDtypeStruct((n, d), F32),
        compiler_params=_params(1),
        name="merge_ffn",
    )(x2d, o_cat, gates, g_ffn, g_fin, *weights)


def _rope_tables(pos):
    inv = ROPE_THETA ** (-jnp.arange(0, ROPE_DIM, 2, dtype=F32) / ROPE_DIM)
    ang = pos.astype(F32)[:, None] * inv[None, :]
    cos, sin = jnp.cos(ang), jnp.sin(ang)
    return jnp.concatenate([cos, cos, cos, cos], axis=1), jnp.concatenate([-sin, sin, -sin, sin], axis=1)


def _swap_halves(w):
    half = w.shape[-1] // 2
    return jnp.concatenate([w[..., half:], w[..., :half]], axis=-1)


def kernel(x_prompt, x_sample, cache_mla_ckv, cache_mla_krope, cache_mem_k, cache_mem_v, mem_prompt, norm_mix_g, w_in, gm_norm_g, gm_ws, gm_bs, mla_q_norm_g, mla_w_uq, mla_kv_norm_g, mla_w_uk, mla_w_uv, mem_norm_g, mem_w_kv, w_br_gm, w_br_mla, w_br_mem, w_out, norm_ffn_g, ffn_w_gate, ffn_w_up, ffn_w_down, final_norm_g):
    depth = w_in.shape[0]
    assert depth == 1, "single-layer step"
    batch, seq, d = x_prompt.shape
    dec_batch, dec_seq, _ = x_sample.shape
    past = cache_mla_ckv.shape[2]
    n_mem = mem_prompt.shape[1]
    q_lora = mla_w_uq.shape[1]
    kv_lora = mla_w_uk.shape[1]
    d_mem = MEM_HEADS * MEM_HEAD_DIM
    assert seq % Q_BLOCK == 0 and seq % INPROJ_BLOCK == 0 and (batch * seq) % MERGE_BLOCK == 0
    assert (dec_batch * dec_seq) % INPROJ_BLOCK == 0 and (dec_batch * dec_seq) % MERGE_BLOCK == 0
    assert INPROJ_BLOCK % dec_seq == 0 and past % GM_CHUNK == 0
    assert past % CHUNK == 0 and dec_seq <= CHUNK, "new tokens must sit in one attention chunk"

    w = w_in[0]
    o = 0
    parts = []
    for width in (d, d, q_lora, kv_lora, ROPE_DIM, d_mem, 3 * d):
        parts.append(w[:, o:o + width])
        o += width
    w_u, w_v, w_cq, w_ckv, w_kr, w_qm, w_gate = parts
    w_c = jnp.concatenate([w_cq, w_kr, _swap_halves(w_kr)], axis=1)
    uq = mla_w_uq[0]
    uq_rope = uq[:, :, NOPE_DIM:]
    w_q3 = jnp.concatenate([uq[:, :, :NOPE_DIM].reshape(q_lora, -1),
                            uq_rope.reshape(q_lora, -1),
                            _swap_halves(uq_rope).reshape(q_lora, -1)], axis=1)
    w_uk = mla_w_uk[0].reshape(kv_lora, -1).astype(BF16)
    w_uv = mla_w_uv[0].reshape(kv_lora, -1).astype(BF16)
    in_weights = [x.astype(BF16) for x in (w_u, w_v, w_c, w_ckv, w_qm, w_gate, w_q3)]
    in_gains = [norm_mix_g[0][None], gm_norm_g[0][None], mla_q_norm_g[0][None], mla_kv_norm_g[0][None]]
    out_weights = [x[0].astype(BF16) for x in (w_br_gm, w_br_mla, w_br_mem, w_out, ffn_w_gate, ffn_w_up, ffn_w_down)]
    gw = gm_ws[0]
    gb = jnp.broadcast_to(gm_bs[0][:, :, None], gm_bs.shape[1:] + (LANE,))

    mk, mv, mkb, mvb = _memkv(mem_prompt.reshape(batch * n_mem, d), mem_norm_g[0][None], mem_w_kv[0].astype(BF16))
    xp = x_prompt.reshape(batch * seq, d)
    cos_p, sin_p = _rope_tables(jnp.arange(seq, dtype=jnp.int32))
    u, vn, q, ckv_p, kr_p, qm, gates, kn, vt, krp = _inproj(xp, cos_p, sin_p, in_gains, in_weights, (w_uk, w_uv.T))
    o_cat = _mix_prompt(batch, seq, u, vn, q, qm, kn, vt, krp, mkb, mvb, gw, gb)
    y_prompt = _merge_ffn(xp, o_cat, gates, norm_ffn_g[0][None], final_norm_g[None], out_weights)

    xs = x_sample.reshape(dec_batch * dec_seq, d)
    cos_s, sin_s = _rope_tables(past + jnp.arange(dec_seq, dtype=jnp.int32))
    reps = max(INPROJ_BLOCK // dec_seq, 1)
    cos_s, sin_s = jnp.tile(cos_s, (reps, 1)), jnp.tile(sin_s, (reps, 1))
    u, vn, q, ckv_s, kr_s, qm, gates, gv_s = _inproj(xs, cos_s, sin_s, in_gains, in_weights)
    o_cat = _mix_sample(dec_batch, dec_seq, past, u, vn, q, qm,
                        cache_mla_ckv[0].reshape(dec_batch * past, kv_lora),
                        cache_mla_krope[0].reshape(dec_batch * past, ROPE_DIM), ckv_s, kr_s, w_uk, w_uv,
                        cache_mem_k[0], cache_mem_v[0], gw, gb)
    y_sample = _merge_ffn(xs, o_cat, gates, norm_ffn_g[0][None], final_norm_g[None], out_weights)

    return (y_prompt.reshape(batch, seq, d), y_sample.reshape(dec_batch, dec_seq, d),
            ckv_p.reshape(1, batch, seq, kv_lora), kr_p.reshape(1, batch, seq, ROPE_DIM),
            mk.reshape(1, batch, n_mem, MEM_HEADS, MEM_HEAD_DIM), mv.reshape(1, batch, n_mem, MEM_HEADS, MEM_HEAD_DIM),
            ckv_s.reshape(1, dec_batch, dec_seq, kv_lora), kr_s.reshape(1, dec_batch, dec_seq, ROPE_DIM),
            gv_s.reshape(1, dec_batch, dec_seq, d))
```

```python
import functools
import math

import jax
import jax.numpy as jnp
from jax import lax
from jax.experimental import pallas as pl
from jax.experimental.pallas import tpu as pltpu

F32 = jnp.float32
BF16 = jnp.bfloat16

NORM_EPS = 1e-6
CHUNK = 64
GM_CHUNK = 128
GM_GROUPS = 8
MLA_HEADS = 8
NOPE_DIM = 128
ROPE_DIM = 64
V_DIM = 128
ROPE_THETA = 10000.0
MEM_HEADS = 4
MEM_HEAD_DIM = 256
LOG2E = math.log2(math.e)
MLA_QSCALE = (NOPE_DIM + ROPE_DIM) ** -0.5 * LOG2E
MEM_QSCALE = MEM_HEAD_DIM ** -0.5 * LOG2E

LANE = 128
HEAD_PAD = 2 * LANE
SUM_ROWS = 16
SOFTMAX_LAG = 2
VALUES_LAG = 4
VMEM_LIMIT_BYTES = 56 * 1024 * 1024

TOKEN_BLOCK = 512
INPROJ_BLOCK = 512
INPROJ_GROUPS = 2
MERGE_BLOCK = 512
MERGE_GROUPS = 2
Q_BLOCK = 512
KEY_SLAB = 256


def _params(n_axes):
    return pltpu.CompilerParams(dimension_semantics=("arbitrary",) * n_axes,
                                vmem_limit_bytes=VMEM_LIMIT_BYTES)


def _resident(shape):
    return pl.BlockSpec(shape, lambda *_: (0,) * len(shape), pipeline_mode=pl.Buffered(1))


def _rows(tm, width):
    return pl.BlockSpec((tm, width), lambda i: (i, 0))


def _dot(a, b):
    return jnp.dot(a, b, preferred_element_type=F32)


def _dot_nt(a, b):
    return lax.dot_general(a, b, (((1,), (1,)), ((), ())), preferred_element_type=F32)


def _rms(x, g):
    return x * lax.rsqrt(jnp.mean(x * x, axis=-1, keepdims=True) + NORM_EPS) * g


def _memkv_kernel(mem_ref, g_ref, w_ref, k_ref, v_ref, kb_ref, vb_ref):
    d = k_ref.shape[1]
    kv = _dot(_rms(mem_ref[...], g_ref[...]).astype(BF16), w_ref[...])
    k, v = kv[:, :d], kv[:, d:]
    k_ref[...] = k
    v_ref[...] = v
    kb_ref[...] = k.astype(BF16)
    vb_ref[...] = v.astype(BF16)


def _memkv(mem2d, g, w_kv):
    n, d = mem2d.shape
    dm = w_kv.shape[1] // 2
    tm = TOKEN_BLOCK
    return pl.pallas_call(
        _memkv_kernel,
        grid=(n // tm,),
        in_specs=[_rows(tm, d), _resident((1, d)), _resident(w_kv.shape)],
        out_specs=[_rows(tm, dm)] * 4,
        out_shape=[jax.ShapeDtypeStruct((n, dm), F32)] * 2 + [jax.ShapeDtypeStruct((n, dm), BF16)] * 2,
        compiler_params=_params(1),
        name="memkv",
    )(mem2d, g, w_kv)


def _inproj_kernel(x_ref, cos_ref, sin_ref, gmix_ref, ggm_ref, gq_ref, gkv_ref,
                   wu_ref, wv_ref, wc_ref, wckv_ref, wqm_ref, wg_ref, wq3_ref, *rest, prompt):
    if prompt:
        wuk_ref, wuvt_ref, u_ref, vn_ref, q_ref, ckv_ref, kr_ref, qm_ref, gate_ref, kn_ref, vt_ref, krp_ref = rest
    else:
        u_ref, vn_ref, q_ref, ckv_ref, kr_ref, qm_ref, gate_ref, vn32_ref = rest
    tm, d = x_ref.shape
    q_lora = wq3_ref.shape[0]
    nn, nr = MLA_HEADS * NOPE_DIM, MLA_HEADS * ROPE_DIM
    low = lax.broadcasted_iota(jnp.int32, (1, LANE), 1) < ROPE_DIM

    def row_group(r):
        xn = _rms(x_ref[r, :], gmix_ref[...]).astype(BF16)
        cos = cos_ref[r, :]
        sin = sin_ref[r, :]

        def gate_chunk(c):
            cols = slice(c * d, (c + 1) * d)
            gate_ref[r, cols] = jax.nn.sigmoid(_dot(xn, wg_ref[:, cols])).astype(BF16)

        zc = _dot(xn, wc_ref[...])
        zkv = _dot(xn, wckv_ref[...])
        zu = _dot(xn, wu_ref[...])
        yield

        cqn = _rms(zc[:, :q_lora], gq_ref[...]).astype(BF16)
        ckv = _rms(zkv, gkv_ref[...])
        ckv_ref[r, :] = ckv
        krk = zc[:, q_lora:]
        krr = krk * jnp.where(low, cos, 0.0) + pltpu.roll(krk, ROPE_DIM, 1) * jnp.where(low, sin, 0.0)
        kr_ref[r, :] = krr[:, :ROPE_DIM]
        q3 = _dot(cqn, wq3_ref[...])
        gate_chunk(0)
        yield

        u_ref[r, :] = jax.nn.gelu(zu).astype(BF16)
        zv = _dot(xn, wv_ref[...])
        if prompt:
            ckvb = ckv.astype(BF16)
            kn_ref[r, :] = _dot(ckvb, wuk_ref[...]).astype(BF16)
            step = min(KEY_SLAB, ckvb.shape[0])
            for lo in range(0, ckvb.shape[0], step):
                k0 = r.start + lo
                vt_ref[k0 // KEY_SLAB, :, k0 % KEY_SLAB:k0 % KEY_SLAB + step] = _dot_nt(
                    wuvt_ref[...], ckvb[lo:lo + step, :]).astype(BF16)
            krp_ref[r, :] = krr.astype(BF16)
        yield

        zeros = jnp.zeros((q3.shape[0], LANE - ROPE_DIM), F32)
        for pair in range(MLA_HEADS // 2):
            lo = pair * LANE
            rot = (q3[:, nn + lo:nn + lo + LANE] * cos + q3[:, nn + nr + lo:nn + nr + lo + LANE] * sin) * MLA_QSCALE
            for h, piece in ((2 * pair, rot[:, :ROPE_DIM]), (2 * pair + 1, rot[:, ROPE_DIM:])):
                q_ref[r, h * HEAD_PAD:h * HEAD_PAD + LANE] = (q3[:, h * LANE:(h + 1) * LANE]
                                                              * MLA_QSCALE).astype(BF16)
                q_ref[r, h * HEAD_PAD + LANE:(h + 1) * HEAD_PAD] = jnp.concatenate([piece, zeros],
                                                                                   axis=1).astype(BF16)
        gate_chunk(1)
        yield

        vn = _rms(jax.nn.gelu(zv), ggm_ref[...])
        vn_ref[r, :] = vn.astype(BF16)
        if not prompt:
            vn32_ref[r, :] = vn
        yield

        qm_ref[r, :] = (_dot(xn, wqm_ref[...]) * MEM_QSCALE).astype(BF16)
        gate_chunk(2)

    rows = tm // INPROJ_GROUPS
    pending = []
    for g in range(INPROJ_GROUPS):
        pending.append(row_group(slice(g * rows, (g + 1) * rows)))
        for gen in list(pending):
            if next(gen, "done") == "done":
                pending.remove(gen)
    while pending:
        for gen in list(pending):
            if next(gen, "done") == "done":
                pending.remove(gen)


def _inproj(x2d, cos_tab, sin_tab, gains, weights, kv_weights=None):
    n, d = x2d.shape
    tm = INPROJ_BLOCK
    prompt = kv_weights is not None
    tab_blocks = cos_tab.shape[0] // tm
    tab_spec = pl.BlockSpec((tm, LANE), lambda i: (i % tab_blocks, 0))
    outs = [(d, BF16), (d, BF16), (MLA_HEADS * HEAD_PAD, BF16), (weights[3].shape[1], F32), (ROPE_DIM, F32),
            (MEM_HEADS * MEM_HEAD_DIM, BF16), (3 * d, BF16)]
    outs += [(MLA_HEADS * NOPE_DIM, BF16), None, (LANE, BF16)] if prompt else [(d, F32)]
    out_specs = [w and _rows(tm, w[0]) for w in outs]
    out_shape = [w and jax.ShapeDtypeStruct((n, w[0]), w[1]) for w in outs]
    if prompt:
        hv = MLA_HEADS * V_DIM
        out_specs[8] = pl.BlockSpec((tm // KEY_SLAB, hv, KEY_SLAB), lambda i: (i, 0, 0))
        out_shape[8] = jax.ShapeDtypeStruct((n // KEY_SLAB, hv, KEY_SLAB), BF16)
        weights = list(weights) + list(kv_weights)
    return pl.pallas_call(
        functools.partial(_inproj_kernel, prompt=prompt),
        grid=(n // tm,),
        in_specs=([_rows(tm, d), tab_spec, tab_spec] + [_resident(g.shape) for g in gains]
                  + [_resident(w.shape) for w in weights]),
        out_specs=out_specs,
        out_shape=out_shape,
        compiler_params=_params(1),
        name="inproj",
    )(x2d, cos_tab, sin_tab, *gains, *weights)


def _tril_weights(gw_ref, g, rows):
    w = gw_ref[g][:rows, :rows]
    r = lax.broadcasted_iota(jnp.int32, (rows, rows), 0)
    c = lax.broadcasted_iota(jnp.int32, (rows, rows), 1)
    return jnp.where(c <= r, w, 0.0).astype(BF16)


def _mem_attend(qm_ref, mk_ref, mv_ref, o_ref, col0):
    for h in range(MEM_HEADS):
        sl = slice(h * MEM_HEAD_DIM, (h + 1) * MEM_HEAD_DIM)
        s = _dot_nt(qm_ref[:, sl], mk_ref[:, sl])
        p = jnp.exp2(s - jnp.max(s, axis=1, keepdims=True))
        o = _dot(p.astype(BF16), mv_ref[:, sl]) / jnp.sum(p, axis=1, keepdims=True)
        o_ref[:, col0 + h * MEM_HEAD_DIM:col0 + (h + 1) * MEM_HEAD_DIM] = o.astype(BF16)


def _mix_prompt_kernel(u_ref, vn_ref, q_ref, qm_ref, kn_ref, vt_ref, krp_ref, mk_ref, mv_ref, gw_ref, gb_ref, o_ref,
                       m_sc, acc_sc):
    qi = pl.program_id(1)
    tq, d = u_ref.shape

    for g in range(GM_GROUPS):
        wg = _tril_weights(gw_ref, g, GM_CHUNK)
        bias = gb_ref[g]
        cs = slice(g * LANE, (g + 1) * LANE)
        for r0 in range(0, tq, 2 * GM_CHUNK):
            ra = slice(r0, r0 + GM_CHUNK)
            rb = slice(r0 + GM_CHUNK, r0 + 2 * GM_CHUNK)
            mixed = _dot(wg, jnp.concatenate([vn_ref[ra, cs], vn_ref[rb, cs]], axis=1))
            o_ref[ra, cs] = (u_ref[ra, cs].astype(F32) * (mixed[:, :LANE] + bias)).astype(BF16)
            o_ref[rb, cs] = (u_ref[rb, cs].astype(F32) * (mixed[:, LANE:] + bias)).astype(BF16)

    half = tq // 2
    kc = lax.broadcasted_iota(jnp.int32, (half, half), 0) // CHUNK
    qc = lax.broadcasted_iota(jnp.int32, (half, half), 1) // CHUNK
    visible = kc <= qc

    ones_rows = jnp.ones((SUM_ROWS, tq), BF16)

    def key_blocks(js, first=False):
        items = [(b, h) for b in range(len(js)) for h in range(MLA_HEADS)]
        scores, probs = {}, {}

        def stage_scores(item):
            b, h = item
            rows = pl.ds(pl.multiple_of(js[b] * tq, tq), tq)
            kh = jnp.concatenate([kn_ref[rows, h * LANE:(h + 1) * LANE], krp_ref[rows, :]], axis=1)
            qh = q_ref[:, h * HEAD_PAD:(h + 1) * HEAD_PAD]
            if first:
                scores[item] = (_dot_nt(kh[:half], qh[:half]), _dot_nt(kh, qh[half:]))
            else:
                scores[item] = _dot_nt(kh, qh)

        def stage_softmax(item):
            h = item[1]
            st = scores.pop(item)
            if first:
                s0, s1 = st
                s0 = jnp.where(visible, s0, -jnp.inf)
                s1 = jnp.concatenate([s1[:half], jnp.where(visible, s1[half:], -jnp.inf)], axis=0)
                m0 = jnp.max(s0, axis=0, keepdims=True)
                m1 = jnp.max(s1, axis=0, keepdims=True)
                m_sc[h] = jnp.concatenate([m0, m1], axis=1)
                probs[item] = (jnp.exp2(s0 - m0).astype(BF16), jnp.exp2(s1 - m1).astype(BF16))
            else:
                m_old = m_sc[h]
                m = jnp.maximum(m_old, jnp.max(st, axis=0, keepdims=True))
                m_sc[h] = m
                probs[item] = (jnp.exp2(st - m).astype(BF16), jnp.exp2(m_old - m))

        def stage_values(item):
            j, h = js[item[0]], item[1]
            slabs = tq // KEY_SLAB
            vt = jnp.concatenate([vt_ref[j * slabs + i, h * V_DIM:(h + 1) * V_DIM, :] for i in range(slabs)], axis=1)
            vt = jnp.concatenate([vt, ones_rows], axis=0)
            if first:
                p0, p1 = probs.pop(item)
                acc_sc[h] = jnp.concatenate([_dot(vt[:, :half], p0), _dot(vt, p1)], axis=1)
            else:
                p, a = probs.pop(item)
                acc_sc[h] = a * acc_sc[h] + _dot(vt, p)

        for step in range(len(items) + VALUES_LAG):
            if step < len(items):
                stage_scores(items[step])
            if 0 <= step - SOFTMAX_LAG < len(items):
                stage_softmax(items[step - SOFTMAX_LAG])
            if 0 <= step - VALUES_LAG < len(items):
                stage_values(items[step - VALUES_LAG])

    key_blocks([qi], first=True)

    def past_block(j, carry):
        key_blocks([j])
        return carry

    lax.fori_loop(0, qi, past_block, 0)
    for h in range(MLA_HEADS):
        acc = acc_sc[h]
        o = acc[:V_DIM, :] * (1.0 / acc[V_DIM:V_DIM + 1, :])
        o_ref[:, d + h * V_DIM:d + (h + 1) * V_DIM] = o.T.astype(BF16)

    _mem_attend(qm_ref, mk_ref, mv_ref, o_ref, d + MLA_HEADS * V_DIM)


def _mix_prompt(batch, seq, u, vn, q, qm, kn, vt, krp, mkb, mvb, gw, gb):
    n, d = u.shape
    tq = Q_BLOCK
    nq = seq // tq
    n_mem = mkb.shape[0] // batch
    blk = lambda w: pl.BlockSpec((tq, w), lambda b, i: (b * nq + i, 0))
    per_batch = lambda rows, w: pl.BlockSpec((rows, w), lambda b, i: (b, 0))
    return pl.pallas_call(
        _mix_prompt_kernel,
        grid=(batch, nq),
        in_specs=[blk(d), blk(d), blk(q.shape[1]), blk(qm.shape[1]),
                  per_batch(seq, kn.shape[1]),
                  pl.BlockSpec((seq // KEY_SLAB,) + vt.shape[1:], lambda b, i: (b, 0, 0)),
                  per_batch(seq, LANE),
                  per_batch(n_mem, mkb.shape[1]), per_batch(n_mem, mvb.shape[1]),
                  _resident(gw.shape), _resident(gb.shape)],
        out_specs=blk(3 * d),
        out_shape=jax.ShapeDtypeStruct((n, 3 * d), BF16),
        scratch_shapes=[pltpu.VMEM((MLA_HEADS, 1, tq), F32), pltpu.VMEM((MLA_HEADS, V_DIM + SUM_ROWS, tq), F32)],
        compiler_params=_params(2),
        name="mix_prompt",
    )(u, vn, q, qm, kn, vt, krp, mkb, mvb, gw, gb)


def _mix_sample_kernel(u_ref, vn_ref, q_ref, qm_ref, ckvp_ref, krp_ref, ckvn_ref, krn_ref, wuk_ref, wuv_ref,
                       mk_ref, mv_ref, gw_ref, gb_ref, o_ref):
    ts, d = u_ref.shape

    for g in range(GM_GROUPS):
        cs = slice(g * LANE, (g + 1) * LANE)
        mixed = _dot(_tril_weights(gw_ref, g, ts), vn_ref[:, cs]) + gb_ref[g][:ts, :]
        o_ref[:, cs] = (u_ref[:, cs].astype(F32) * mixed).astype(BF16)

    def keys(c_ref, r_ref):
        r = r_ref[...]
        c = c_ref[...].astype(BF16)
        return c, jnp.concatenate([c, jnp.concatenate([r, jnp.zeros_like(r)], axis=1).astype(BF16)], axis=1)

    cp, kp = keys(ckvp_ref, krp_ref)
    cn, kn = keys(ckvn_ref, krn_ref)
    qs = []
    for h in range(MLA_HEADS):
        q_abs = _dot_nt(q_ref[:, h * HEAD_PAD:h * HEAD_PAD + NOPE_DIM], wuk_ref[:, h * NOPE_DIM:(h + 1) * NOPE_DIM])
        qs.append(jnp.concatenate([q_abs.astype(BF16), q_ref[:, h * HEAD_PAD + NOPE_DIM:(h + 1) * HEAD_PAD]], axis=1))
    qa = jnp.concatenate(qs, axis=0)
    s_past = _dot_nt(qa, kp)
    s_new = _dot_nt(qa, kn)
    m = jnp.maximum(jnp.max(s_past, axis=1, keepdims=True), jnp.max(s_new, axis=1, keepdims=True))
    p_past = jnp.exp2(s_past - m)
    p_new = jnp.exp2(s_new - m)
    l = jnp.sum(p_past, axis=1, keepdims=True) + jnp.sum(p_new, axis=1, keepdims=True)
    lat = ((_dot(p_past.astype(BF16), cp) + _dot(p_new.astype(BF16), cn)) / l).astype(BF16)
    for h in range(MLA_HEADS):
        o_ref[:, d + h * V_DIM:d + (h + 1) * V_DIM] = _dot(lat[h * ts:(h + 1) * ts, :],
                                                          wuv_ref[:, h * V_DIM:(h + 1) * V_DIM]).astype(BF16)

    _mem_attend(qm_ref, mk_ref, mv_ref, o_ref, d + MLA_HEADS * V_DIM)


def _mix_sample(batch, ts, past, u, vn, q, qm, ckv_past, kr_past, ckv_new, kr_new, w_uk, w_uv, mkb, mvb, gw, gb):
    n, d = u.shape
    n_mem = mkb.shape[0] // batch
    per_batch = lambda rows, w: pl.BlockSpec((rows, w), lambda b: (b, 0))
    return pl.pallas_call(
        _mix_sample_kernel,
        grid=(batch,),
        in_specs=[per_batch(ts, d), per_batch(ts, d), per_batch(ts, q.shape[1]), per_batch(ts, qm.shape[1]),
                  per_batch(past, ckv_past.shape[1]), per_batch(past, ROPE_DIM),
                  per_batch(ts, ckv_new.shape[1]), per_batch(ts, ROPE_DIM),
                  _resident(w_uk.shape), _resident(w_uv.shape),
                  per_batch(n_mem, mkb.shape[1]), per_batch(n_mem, mvb.shape[1]),
                  _resident(gw.shape), _resident(gb.shape)],
        out_specs=per_batch(ts, 3 * d),
        out_shape=jax.ShapeDtypeStruct((n, 3 * d), BF16),
        compiler_params=_params(1),
        name="mix_sample",
    )(u, vn, q, qm, ckv_past, kr_past, ckv_new, kr_new, w_uk, w_uv, mkb, mvb, gw, gb)


def _merge_ffn_kernel(x_ref, o_ref, gate_ref, gffn_ref, gfin_ref, wgm_ref, wmla_ref, wmem_ref, wout_ref,
                      wfg_ref, wfu_ref, wfd_ref, y_ref):
    tm, d = x_ref.shape

    def merge(r):
        merged = (gate_ref[r, 0:d].astype(F32) * _dot(o_ref[r, 0:d], wgm_ref[...])
                  + gate_ref[r, d:2 * d].astype(F32) * _dot(o_ref[r, d:2 * d], wmla_ref[...])
                  + gate_ref[r, 2 * d:3 * d].astype(F32) * _dot(o_ref[r, 2 * d:3 * d], wmem_ref[...]))
        return x_ref[r, :] + _dot(merged.astype(BF16), wout_ref[...])

    def ffn_hidden(h):
        hn = _rms(h, gffn_ref[...]).astype(BF16)
        return (jax.nn.silu(_dot(hn, wfg_ref[...])) * _dot(hn, wfu_ref[...])).astype(BF16)

    def finish(r, h, a):
        y_ref[r, :] = _rms(h + _dot(a, wfd_ref[...]), gfin_ref[...])

    groups = [slice(i * tm // MERGE_GROUPS, (i + 1) * tm // MERGE_GROUPS) for i in range(MERGE_GROUPS)]
    hs = [merge(r) for r in groups]
    acts = [ffn_hidden(h) for h in hs]
    for r, h, a in zip(groups, hs, acts):
        finish(r, h, a)


def _merge_ffn(x2d, o_cat, gates, g_ffn, g_fin, weights):
    n, d = x2d.shape
    tm = MERGE_BLOCK
    return pl.pallas_call(
        _merge_ffn_kernel,
        grid=(n // tm,),
        in_specs=([_rows(tm, d), _rows(tm, 3 * d), _rows(tm, 3 * d), _resident(g_ffn.shape), _resident(g_fin.shape)]
                  + [_resident(w.shape) for w in weights]),
        out_specs=_rows(tm, d),
        out_shape=jax.ShapeDtypeStruct((n, d), F32),
        compiler_params=_params(1),
        name="merge_ffn",
    )(x2d, o_cat, gates, g_ffn, g_fin, *weights)


def _rope_tables(pos):
    inv = ROPE_THETA ** (-jnp.arange(0, ROPE_DIM, 2, dtype=F32) / ROPE_DIM)
    ang = pos.astype(F32)[:, None] * inv[None, :]
    cos, sin = jnp.cos(ang), jnp.sin(ang)
    return jnp.concatenate([cos, cos, cos, cos], axis=1), jnp.concatenate([-sin, sin, -sin, sin], axis=1)


def _swap_halves(w):
    half = w.shape[-1] // 2
    return jnp.concatenate([w[..., half:], w[..., :half]], axis=-1)


def kernel(x_prompt, x_sample, cache_mla_ckv, cache_mla_krope, cache_mem_k, cache_mem_v, mem_prompt, norm_mix_g, w_in, gm_norm_g, gm_ws, gm_bs, mla_q_norm_g, mla_w_uq, mla_kv_norm_g, mla_w_uk, mla_w_uv, mem_norm_g, mem_w_kv, w_br_gm, w_br_mla, w_br_mem, w_out, norm_ffn_g, ffn_w_gate, ffn_w_up, ffn_w_down, final_norm_g):
    depth = w_in.shape[0]
    assert depth == 1, "single-layer step"
    batch, seq, d = x_prompt.shape
    dec_batch, dec_seq, _ = x_sample.shape
    past = cache_mla_ckv.shape[2]
    n_mem = mem_prompt.shape[1]
    q_lora = mla_w_uq.shape[1]
    kv_lora = mla_w_uk.shape[1]
    d_mem = MEM_HEADS * MEM_HEAD_DIM
    assert seq % Q_BLOCK == 0 and seq % INPROJ_BLOCK == 0 and (batch * seq) % MERGE_BLOCK == 0
    assert (dec_batch * dec_seq) % INPROJ_BLOCK == 0 and (dec_batch * dec_seq) % MERGE_BLOCK == 0
    assert INPROJ_BLOCK % dec_seq == 0 and past % GM_CHUNK == 0 and (batch * n_mem) % TOKEN_BLOCK == 0
    assert past % CHUNK == 0 and dec_seq <= CHUNK, "new tokens must sit in one attention chunk"

    w = w_in[0]
    o = 0
    parts = []
    for width in (d, d, q_lora, kv_lora, ROPE_DIM, d_mem, 3 * d):
        parts.append(w[:, o:o + width])
        o += width
    w_u, w_v, w_cq, w_ckv, w_kr, w_qm, w_gate = parts
    w_c = jnp.concatenate([w_cq, w_kr, _swap_halves(w_kr)], axis=1)
    uq = mla_w_uq[0]
    uq_rope = uq[:, :, NOPE_DIM:]
    w_q3 = jnp.concatenate([uq[:, :, :NOPE_DIM].reshape(q_lora, -1),
                            uq_rope.reshape(q_lora, -1),
                            _swap_halves(uq_rope).reshape(q_lora, -1)], axis=1)
    w_uk = mla_w_uk[0].reshape(kv_lora, -1).astype(BF16)
    w_uv = mla_w_uv[0].reshape(kv_lora, -1).astype(BF16)
    in_weights = [x.astype(BF16) for x in (w_u, w_v, w_c, w_ckv, w_qm, w_gate, w_q3)]
    in_gains = [norm_mix_g[0][None], gm_norm_g[0][None], mla_q_norm_g[0][None], mla_kv_norm_g[0][None]]
    out_weights = [x[0].astype(BF16) for x in (w_br_gm, w_br_mla, w_br_mem, w_out, ffn_w_gate, ffn_w_up, ffn_w_down)]
    gw = gm_ws[0]
    gb = jnp.broadcast_to(gm_bs[0][:, :, None], gm_bs.shape[1:] + (LANE,))

    mk, mv, mkb, mvb = _memkv(mem_prompt.reshape(batch * n_mem, d), mem_norm_g[0][None], mem_w_kv[0].astype(BF16))
    xp = x_prompt.reshape(batch * seq, d)
    cos_p, sin_p = _rope_tables(jnp.arange(seq, dtype=jnp.int32))
    u, vn, q, ckv_p, kr_p, qm, gates, kn, vt, krp = _inproj(xp, cos_p, sin_p, in_gains, in_weights, (w_uk, w_uv.T))
    o_cat = _mix_prompt(batch, seq, u, vn, q, qm, kn, vt, krp, mkb, mvb, gw, gb)
    y_prompt = _merge_ffn(xp, o_cat, gates, norm_ffn_g[0][None], final_norm_g[None], out_weights)

    xs = x_sample.reshape(dec_batch * dec_seq, d)
    cos_s, sin_s = _rope_tables(past + jnp.arange(dec_seq, dtype=jnp.int32))
    reps = max(INPROJ_BLOCK // dec_seq, 1)
    cos_s, sin_s = jnp.tile(cos_s, (reps, 1)), jnp.tile(sin_s, (reps, 1))
    u, vn, q, ckv_s, kr_s, qm, gates, gv_s = _inproj(xs, cos_s, sin_s, in_gains, in_weights)
    cmk = cache_mem_k[0].reshape(dec_batch * n_mem, d_mem).astype(BF16)
    cmv = cache_mem_v[0].reshape(dec_batch * n_mem, d_mem).astype(BF16)
    o_cat = _mix_sample(dec_batch, dec_seq, past, u, vn, q, qm,
                        cache_mla_ckv[0].reshape(dec_batch * past, kv_lora),
                        cache_mla_krope[0].reshape(dec_batch * past, ROPE_DIM), ckv_s, kr_s, w_uk, w_uv,
                        cmk, cmv, gw, gb)
    y_sample = _merge_ffn(xs, o_cat, gates, norm_ffn_g[0][None], final_norm_g[None], out_weights)

    return (y_prompt.reshape(batch, seq, d), y_sample.reshape(dec_batch, dec_seq, d),
            ckv_p.reshape(1, batch, seq, kv_lora), kr_p.reshape(1, batch, seq, ROPE_DIM),
            mk.reshape(1, batch, n_mem, MEM_HEADS, MEM_HEAD_DIM), mv.reshape(1, batch, n_mem, MEM_HEADS, MEM_HEAD_DIM),
            ckv_s.reshape(1, dec_batch, dec_seq, kv_lora), kr_s.reshape(1, dec_batch, dec_seq, ROPE_DIM),
            gv_s.reshape(1, dec_batch, dec_seq, d))
```

```python
import functools
import math

import jax
import jax.numpy as jnp
from jax import lax
from jax.experimental import pallas as pl
from jax.experimental.pallas import tpu as pltpu

F32 = jnp.float32
BF16 = jnp.bfloat16

NORM_EPS = 1e-6
CHUNK = 64
GM_CHUNK = 128
GM_GROUPS = 8
MLA_HEADS = 8
NOPE_DIM = 128
ROPE_DIM = 64
V_DIM = 128
ROPE_THETA = 10000.0
MEM_HEADS = 4
MEM_HEAD_DIM = 256
LOG2E = math.log2(math.e)
MLA_QSCALE = (NOPE_DIM + ROPE_DIM) ** -0.5 * LOG2E
MEM_QSCALE = MEM_HEAD_DIM ** -0.5 * LOG2E

LANE = 128
HEAD_PAD = 2 * LANE
SUM_ROWS = 16
SOFTMAX_LAG = 2
VALUES_LAG = 4
VMEM_LIMIT_BYTES = 56 * 1024 * 1024

TOKEN_BLOCK = 512
INPROJ_BLOCK = 512
INPROJ_GROUPS = 2
MERGE_BLOCK = 512
MERGE_GROUPS = 2
Q_BLOCK = 512
KEY_SLAB = 256


def _params(n_axes):
    return pltpu.CompilerParams(dimension_semantics=("arbitrary",) * n_axes,
                                vmem_limit_bytes=VMEM_LIMIT_BYTES)


def _resident(shape):
    return pl.BlockSpec(shape, lambda *_: (0,) * len(shape), pipeline_mode=pl.Buffered(1))


def _rows(tm, width):
    return pl.BlockSpec((tm, width), lambda i: (i, 0))


def _dot(a, b):
    return jnp.dot(a, b, preferred_element_type=F32)


def _dot_nt(a, b):
    return lax.dot_general(a, b, (((1,), (1,)), ((), ())), preferred_element_type=F32)


def _rms(x, g):
    return x * lax.rsqrt(jnp.mean(x * x, axis=-1, keepdims=True) + NORM_EPS) * g


def _memkv_kernel(mem_ref, g_ref, w_ref, k_ref, v_ref, kb_ref, vb_ref):
    d = kb_ref.shape[1]
    kv = _dot(_rms(mem_ref[...], g_ref[...]).astype(BF16), w_ref[...])
    k, v = kv[:, :d], kv[:, d:]
    for h in range(MEM_HEADS):
        k_ref[:, h, :] = k[:, h * MEM_HEAD_DIM:(h + 1) * MEM_HEAD_DIM]
        v_ref[:, h, :] = v[:, h * MEM_HEAD_DIM:(h + 1) * MEM_HEAD_DIM]
    kb_ref[...] = k.astype(BF16)
    vb_ref[...] = v.astype(BF16)


def _memkv(mem2d, g, w_kv):
    n, d = mem2d.shape
    dm = w_kv.shape[1] // 2
    tm = TOKEN_BLOCK
    return pl.pallas_call(
        _memkv_kernel,
        grid=(n // tm,),
        in_specs=[_rows(tm, d), _resident((1, d)), _resident(w_kv.shape)],
        out_specs=[pl.BlockSpec((tm, MEM_HEADS, MEM_HEAD_DIM), lambda i: (i, 0, 0))] * 2 + [_rows(tm, dm)] * 2,
        out_shape=([jax.ShapeDtypeStruct((n, MEM_HEADS, MEM_HEAD_DIM), F32)] * 2
                   + [jax.ShapeDtypeStruct((n, dm), BF16)] * 2),
        compiler_params=_params(1),
        name="memkv",
    )(mem2d, g, w_kv)


def _inproj_kernel(x_ref, cos_ref, sin_ref, gmix_ref, ggm_ref, gq_ref, gkv_ref,
                   wu_ref, wv_ref, wc_ref, wckv_ref, wqm_ref, wg_ref, wq3_ref, *rest, prompt):
    if prompt:
        wuk_ref, wuvt_ref, u_ref, vn_ref, q_ref, ckv_ref, kr_ref, qm_ref, gate_ref, kn_ref, vt_ref, krp_ref = rest
    else:
        u_ref, vn_ref, q_ref, ckv_ref, kr_ref, qm_ref, gate_ref, vn32_ref = rest
    tm, d = x_ref.shape
    q_lora = wq3_ref.shape[0]
    nn, nr = MLA_HEADS * NOPE_DIM, MLA_HEADS * ROPE_DIM
    low = lax.broadcasted_iota(jnp.int32, (1, LANE), 1) < ROPE_DIM

    def row_group(r):
        xn = _rms(x_ref[r, :], gmix_ref[...]).astype(BF16)
        cos = cos_ref[r, :]
        sin = sin_ref[r, :]

        def gate_chunk(c):
            cols = slice(c * d, (c + 1) * d)
            gate_ref[r, cols] = jax.nn.sigmoid(_dot(xn, wg_ref[:, cols])).astype(BF16)

        zc = _dot(xn, wc_ref[...])
        zkv = _dot(xn, wckv_ref[...])
        zu = _dot(xn, wu_ref[...])
        yield

        cqn = _rms(zc[:, :q_lora], gq_ref[...]).astype(BF16)
        ckv = _rms(zkv, gkv_ref[...])
        ckv_ref[r, :] = ckv
        krk = zc[:, q_lora:]
        krr = krk * jnp.where(low, cos, 0.0) + pltpu.roll(krk, ROPE_DIM, 1) * jnp.where(low, sin, 0.0)
        kr_ref[r, :] = krr[:, :ROPE_DIM]
        q3 = _dot(cqn, wq3_ref[...])
        gate_chunk(0)
        yield

        u_ref[r, :] = jax.nn.gelu(zu).astype(BF16)
        zv = _dot(xn, wv_ref[...])
        if prompt:
            ckvb = ckv.astype(BF16)
            kn_ref[r, :] = _dot(ckvb, wuk_ref[...]).astype(BF16)
            step = min(KEY_SLAB, ckvb.shape[0])
            for lo in range(0, ckvb.shape[0], step):
                k0 = r.start + lo
                vt_ref[k0 // KEY_SLAB, :, k0 % KEY_SLAB:k0 % KEY_SLAB + step] = _dot_nt(
                    wuvt_ref[...], ckvb[lo:lo + step, :]).astype(BF16)
            krp_ref[r, :] = krr.astype(BF16)
        yield

        zeros = jnp.zeros((q3.shape[0], LANE - ROPE_DIM), F32)
        for pair in range(MLA_HEADS // 2):
            lo = pair * LANE
            rot = (q3[:, nn + lo:nn + lo + LANE] * cos + q3[:, nn + nr + lo:nn + nr + lo + LANE] * sin) * MLA_QSCALE
            for h, piece in ((2 * pair, rot[:, :ROPE_DIM]), (2 * pair + 1, rot[:, ROPE_DIM:])):
                q_ref[r, h * HEAD_PAD:h * HEAD_PAD + LANE] = (q3[:, h * LANE:(h + 1) * LANE]
                                                              * MLA_QSCALE).astype(BF16)
                q_ref[r, h * HEAD_PAD + LANE:(h + 1) * HEAD_PAD] = jnp.concatenate([piece, zeros],
                                                                                   axis=1).astype(BF16)
        gate_chunk(1)
        yield

        vn = _rms(jax.nn.gelu(zv), ggm_ref[...])
        vn_ref[r, :] = vn.astype(BF16)
        if not prompt:
            vn32_ref[r, :] = vn
        yield

        qm_ref[r, :] = (_dot(xn, wqm_ref[...]) * MEM_QSCALE).astype(BF16)
        gate_chunk(2)

    rows = tm // INPROJ_GROUPS
    pending = []
    for g in range(INPROJ_GROUPS):
        pending.append(row_group(slice(g * rows, (g + 1) * rows)))
        for gen in list(pending):
            if next(gen, "done") == "done":
                pending.remove(gen)
    while pending:
        for gen in list(pending):
            if next(gen, "done") == "done":
                pending.remove(gen)


def _inproj(x2d, cos_tab, sin_tab, gains, weights, kv_weights=None):
    n, d = x2d.shape
    tm = INPROJ_BLOCK
    prompt = kv_weights is not None
    tab_blocks = cos_tab.shape[0] // tm
    tab_spec = pl.BlockSpec((tm, LANE), lambda i: (i % tab_blocks, 0))
    outs = [(d, BF16), (d, BF16), (MLA_HEADS * HEAD_PAD, BF16), (weights[3].shape[1], F32), (ROPE_DIM, F32),
            (MEM_HEADS * MEM_HEAD_DIM, BF16), (3 * d, BF16)]
    outs += [(MLA_HEADS * NOPE_DIM, BF16), None, (LANE, BF16)] if prompt else [(d, F32)]
    out_specs = [w and _rows(tm, w[0]) for w in outs]
    out_shape = [w and jax.ShapeDtypeStruct((n, w[0]), w[1]) for w in outs]
    if prompt:
        hv = MLA_HEADS * V_DIM
        out_specs[8] = pl.BlockSpec((tm // KEY_SLAB, hv, KEY_SLAB), lambda i: (i, 0, 0))
        out_shape[8] = jax.ShapeDtypeStruct((n // KEY_SLAB, hv, KEY_SLAB), BF16)
        weights = list(weights) + list(kv_weights)
    return pl.pallas_call(
        functools.partial(_inproj_kernel, prompt=prompt),
        grid=(n // tm,),
        in_specs=([_rows(tm, d), tab_spec, tab_spec] + [_resident(g.shape) for g in gains]
                  + [_resident(w.shape) for w in weights]),
        out_specs=out_specs,
        out_shape=out_shape,
        compiler_params=_params(1),
        name="inproj",
    )(x2d, cos_tab, sin_tab, *gains, *weights)


def _tril_weights(gw_ref, g, rows):
    w = gw_ref[g][:rows, :rows]
    r = lax.broadcasted_iota(jnp.int32, (rows, rows), 0)
    c = lax.broadcasted_iota(jnp.int32, (rows, rows), 1)
    return jnp.where(c <= r, w, 0.0).astype(BF16)


def _mem_attend(qm_ref, mk_ref, mv_ref, o_ref, col0):
    for h in range(MEM_HEADS):
        sl = slice(h * MEM_HEAD_DIM, (h + 1) * MEM_HEAD_DIM)
        s = _dot_nt(qm_ref[:, sl], mk_ref[:, sl])
        p = jnp.exp2(s - jnp.max(s, axis=1, keepdims=True))
        o = _dot(p.astype(BF16), mv_ref[:, sl]) / jnp.sum(p, axis=1, keepdims=True)
        o_ref[:, col0 + h * MEM_HEAD_DIM:col0 + (h + 1) * MEM_HEAD_DIM] = o.astype(BF16)


def _mix_prompt_kernel(u_ref, vn_ref, q_ref, qm_ref, kn_ref, vt_ref, krp_ref, mk_ref, mv_ref, gw_ref, gb_ref, o_ref,
                       m_sc, acc_sc):
    qi = pl.program_id(1)
    tq, d = u_ref.shape

    for g in range(GM_GROUPS):
        wg = _tril_weights(gw_ref, g, GM_CHUNK)
        bias = gb_ref[g]
        cs = slice(g * LANE, (g + 1) * LANE)
        for r0 in range(0, tq, 2 * GM_CHUNK):
            ra = slice(r0, r0 + GM_CHUNK)
            rb = slice(r0 + GM_CHUNK, r0 + 2 * GM_CHUNK)
            mixed = _dot(wg, jnp.concatenate([vn_ref[ra, cs], vn_ref[rb, cs]], axis=1))
            o_ref[ra, cs] = (u_ref[ra, cs].astype(F32) * (mixed[:, :LANE] + bias)).astype(BF16)
            o_ref[rb, cs] = (u_ref[rb, cs].astype(F32) * (mixed[:, LANE:] + bias)).astype(BF16)

    half = tq // 2
    kc = lax.broadcasted_iota(jnp.int32, (half, half), 0) // CHUNK
    qc = lax.broadcasted_iota(jnp.int32, (half, half), 1) // CHUNK
    visible = kc <= qc

    ones_rows = jnp.ones((SUM_ROWS, tq), BF16)

    def key_blocks(js, first=False):
        items = [(b, h) for b in range(len(js)) for h in range(MLA_HEADS)]
        scores, probs = {}, {}

        def stage_scores(item):
            b, h = item
            rows = pl.ds(pl.multiple_of(js[b] * tq, tq), tq)
            kh = jnp.concatenate([kn_ref[rows, h * LANE:(h + 1) * LANE], krp_ref[rows, :]], axis=1)
            qh = q_ref[:, h * HEAD_PAD:(h + 1) * HEAD_PAD]
            if first:
                scores[item] = (_dot_nt(kh[:half], qh[:half]), _dot_nt(kh, qh[half:]))
            else:
                scores[item] = _dot_nt(kh, qh)

        def stage_softmax(item):
            h = item[1]
            st = scores.pop(item)
            if first:
                s0, s1 = st
                s0 = jnp.where(visible, s0, -jnp.inf)
                s1 = jnp.concatenate([s1[:half], jnp.where(visible, s1[half:], -jnp.inf)], axis=0)
                m0 = jnp.max(s0, axis=0, keepdims=True)
                m1 = jnp.max(s1, axis=0, keepdims=True)
                m_sc[h] = jnp.concatenate([m0, m1], axis=1)
                probs[item] = (jnp.exp2(s0 - m0).astype(BF16), jnp.exp2(s1 - m1).astype(BF16))
            else:
                m_old = m_sc[h]
                m = jnp.maximum(m_old, jnp.max(st, axis=0, keepdims=True))
                m_sc[h] = m
                probs[item] = (jnp.exp2(st - m).astype(BF16), jnp.exp2(m_old - m))

        def stage_values(item):
            j, h = js[item[0]], item[1]
            slabs = tq // KEY_SLAB
            vt = jnp.concatenate([vt_ref[j * slabs + i, h * V_DIM:(h + 1) * V_DIM, :] for i in range(slabs)], axis=1)
            vt = jnp.concatenate([vt, ones_rows], axis=0)
            if first:
                p0, p1 = probs.pop(item)
                acc_sc[h] = jnp.concatenate([_dot(vt[:, :half], p0), _dot(vt, p1)], axis=1)
            else:
                p, a = probs.pop(item)
                acc_sc[h] = a * acc_sc[h] + _dot(vt, p)

        for step in range(len(items) + VALUES_LAG):
            if step < len(items):
                stage_scores(items[step])
            if 0 <= step - SOFTMAX_LAG < len(items):
                stage_softmax(items[step - SOFTMAX_LAG])
            if 0 <= step - VALUES_LAG < len(items):
                stage_values(items[step - VALUES_LAG])

    key_blocks([qi], first=True)

    def past_block(j, carry):
        key_blocks([j])
        return carry

    lax.fori_loop(0, qi, past_block, 0)
    for h in range(MLA_HEADS):
        acc = acc_sc[h]
        o = acc[:V_DIM, :] * (1.0 / acc[V_DIM:V_DIM + 1, :])
        o_ref[:, d + h * V_DIM:d + (h + 1) * V_DIM] = o.T.astype(BF16)

    _mem_attend(qm_ref, mk_ref, mv_ref, o_ref, d + MLA_HEADS * V_DIM)


def _mix_prompt(batch, seq, u, vn, q, qm, kn, vt, krp, mkb, mvb, gw, gb):
    n, d = u.shape
    tq = Q_BLOCK
    nq = seq // tq
    n_mem = mkb.shape[0] // batch
    blk = lambda w: pl.BlockSpec((tq, w), lambda b, i: (b * nq + i, 0))
    per_batch = lambda rows, w: pl.BlockSpec((rows, w), lambda b, i: (b, 0))
    return pl.pallas_call(
        _mix_prompt_kernel,
        grid=(batch, nq),
        in_specs=[blk(d), blk(d), blk(q.shape[1]), blk(qm.shape[1]),
                  per_batch(seq, kn.shape[1]),
                  pl.BlockSpec((seq // KEY_SLAB,) + vt.shape[1:], lambda b, i: (b, 0, 0)),
                  per_batch(seq, LANE),
                  per_batch(n_mem, mkb.shape[1]), per_batch(n_mem, mvb.shape[1]),
                  _resident(gw.shape), _resident(gb.shape)],
        out_specs=blk(3 * d),
        out_shape=jax.ShapeDtypeStruct((n, 3 * d), BF16),
        scratch_shapes=[pltpu.VMEM((MLA_HEADS, 1, tq), F32), pltpu.VMEM((MLA_HEADS, V_DIM + SUM_ROWS, tq), F32)],
        compiler_params=_params(2),
        name="mix_prompt",
    )(u, vn, q, qm, kn, vt, krp, mkb, mvb, gw, gb)


def _mix_sample_kernel(u_ref, vn_ref, q_ref, qm_ref, ckvp_ref, krp_ref, ckvn_ref, krn_ref, wuk_ref, wuv_ref,
                       mk_ref, mv_ref, gw_ref, gb_ref, o_ref):
    ts, d = u_ref.shape

    for g in range(GM_GROUPS):
        cs = slice(g * LANE, (g + 1) * LANE)
        mixed = _dot(_tril_weights(gw_ref, g, ts), vn_ref[:, cs]) + gb_ref[g][:ts, :]
        o_ref[:, cs] = (u_ref[:, cs].astype(F32) * mixed).astype(BF16)

    def keys(c_ref, r_ref):
        r = r_ref[...]
        c = c_ref[...].astype(BF16)
        return c, jnp.concatenate([c, jnp.concatenate([r, jnp.zeros_like(r)], axis=1).astype(BF16)], axis=1)

    cp, kp = keys(ckvp_ref, krp_ref)
    cn, kn = keys(ckvn_ref, krn_ref)
    qs = []
    for h in range(MLA_HEADS):
        q_abs = _dot_nt(q_ref[:, h * HEAD_PAD:h * HEAD_PAD + NOPE_DIM], wuk_ref[:, h * NOPE_DIM:(h + 1) * NOPE_DIM])
        qs.append(jnp.concatenate([q_abs.astype(BF16), q_ref[:, h * HEAD_PAD + NOPE_DIM:(h + 1) * HEAD_PAD]], axis=1))
    qa = jnp.concatenate(qs, axis=0)
    s_past = _dot_nt(qa, kp)
    s_new = _dot_nt(qa, kn)
    m = jnp.maximum(jnp.max(s_past, axis=1, keepdims=True), jnp.max(s_new, axis=1, keepdims=True))
    p_past = jnp.exp2(s_past - m)
    p_new = jnp.exp2(s_new - m)
    l = jnp.sum(p_past, axis=1, keepdims=True) + jnp.sum(p_new, axis=1, keepdims=True)
    lat = ((_dot(p_past.astype(BF16), cp) + _dot(p_new.astype(BF16), cn)) / l).astype(BF16)
    for h in range(MLA_HEADS):
        o_ref[:, d + h * V_DIM:d + (h + 1) * V_DIM] = _dot(lat[h * ts:(h + 1) * ts, :],
                                                          wuv_ref[:, h * V_DIM:(h + 1) * V_DIM]).astype(BF16)

    _mem_attend(qm_ref, mk_ref, mv_ref, o_ref, d + MLA_HEADS * V_DIM)


def _mix_sample(batch, ts, past, u, vn, q, qm, ckv_past, kr_past, ckv_new, kr_new, w_uk, w_uv, mkb, mvb, gw, gb):
    n, d = u.shape
    n_mem = mkb.shape[0] // batch
    per_batch = lambda rows, w: pl.BlockSpec((rows, w), lambda b: (b, 0))
    return pl.pallas_call(
        _mix_sample_kernel,
        grid=(batch,),
        in_specs=[per_batch(ts, d), per_batch(ts, d), per_batch(ts, q.shape[1]), per_batch(ts, qm.shape[1]),
                  per_batch(past, ckv_past.shape[1]), per_batch(past, ROPE_DIM),
                  per_batch(ts, ckv_new.shape[1]), per_batch(ts, ROPE_DIM),
                  _resident(w_uk.shape), _resident(w_uv.shape),
                  per_batch(n_mem, mkb.shape[1]), per_batch(n_mem, mvb.shape[1]),
                  _resident(gw.shape), _resident(gb.shape)],
        out_specs=per_batch(ts, 3 * d),
        out_shape=jax.ShapeDtypeStruct((n, 3 * d), BF16),
        compiler_params=_params(1),
        name="mix_sample",
    )(u, vn, q, qm, ckv_past, kr_past, ckv_new, kr_new, w_uk, w_uv, mkb, mvb, gw, gb)


def _merge_ffn_kernel(x_ref, o_ref, gate_ref, gffn_ref, gfin_ref, wgm_ref, wmla_ref, wmem_ref, wout_ref,
                      wfg_ref, wfu_ref, wfd_ref, y_ref):
    tm, d = x_ref.shape

    def merge(r):
        merged = (gate_ref[r, 0:d].astype(F32) * _dot(o_ref[r, 0:d], wgm_ref[...])
                  + gate_ref[r, d:2 * d].astype(F32) * _dot(o_ref[r, d:2 * d], wmla_ref[...])
                  + gate_ref[r, 2 * d:3 * d].astype(F32) * _dot(o_ref[r, 2 * d:3 * d], wmem_ref[...]))
        return x_ref[r, :] + _dot(merged.astype(BF16), wout_ref[...])

    def ffn_hidden(h):
        hn = _rms(h, gffn_ref[...]).astype(BF16)
        return (jax.nn.silu(_dot(hn, wfg_ref[...])) * _dot(hn, wfu_ref[...])).astype(BF16)

    def finish(r, h, a):
        y_ref[r, :] = _rms(h + _dot(a, wfd_ref[...]), gfin_ref[...])

    groups = [slice(i * tm // MERGE_GROUPS, (i + 1) * tm // MERGE_GROUPS) for i in range(MERGE_GROUPS)]
    hs = [merge(r) for r in groups]
    acts = [ffn_hidden(h) for h in hs]
    for r, h, a in zip(groups, hs, acts):
        finish(r, h, a)


def _merge_ffn(x2d, o_cat, gates, g_ffn, g_fin, weights):
    n, d = x2d.shape
    tm = MERGE_BLOCK
    return pl.pallas_call(
        _merge_ffn_kernel,
        grid=(n // tm,),
        in_specs=([_rows(tm, d), _rows(tm, 3 * d), _rows(tm, 3 * d), _resident(g_ffn.shape), _resident(g_fin.shape)]
                  + [_resident(w.shape) for w in weights]),
        out_specs=_rows(tm, d),
        out_shape=jax.ShapeDtypeStruct((n, d), F32),
        compiler_params=_params(1),
        name="merge_ffn",
    )(x2d, o_cat, gates, g_ffn, g_fin, *weights)


def _rope_tables(pos):
    inv = ROPE_THETA ** (-jnp.arange(0, ROPE_DIM, 2, dtype=F32) / ROPE_DIM)
    ang = pos.astype(F32)[:, None] * inv[None, :]
    cos, sin = jnp.cos(ang), jnp.sin(ang)
    return jnp.concatenate([cos, cos, cos, cos], axis=1), jnp.concatenate([-sin, sin, -sin, sin], axis=1)


def _swap_halves(w):
    half = w.shape[-1] // 2
    return jnp.concatenate([w[..., half:], w[..., :half]], axis=-1)


def kernel(x_prompt, x_sample, cache_mla_ckv, cache_mla_krope, cache_mem_k, cache_mem_v, mem_prompt, norm_mix_g, w_in, gm_norm_g, gm_ws, gm_bs, mla_q_norm_g, mla_w_uq, mla_kv_norm_g, mla_w_uk, mla_w_uv, mem_norm_g, mem_w_kv, w_br_gm, w_br_mla, w_br_mem, w_out, norm_ffn_g, ffn_w_gate, ffn_w_up, ffn_w_down, final_norm_g):
    depth = w_in.shape[0]
    assert depth == 1, "single-layer step"
    batch, seq, d = x_prompt.shape
    dec_batch, dec_seq, _ = x_sample.shape
    past = cache_mla_ckv.shape[2]
    n_mem = mem_prompt.shape[1]
    q_lora = mla_w_uq.shape[1]
    kv_lora = mla_w_uk.shape[1]
    d_mem = MEM_HEADS * MEM_HEAD_DIM
    assert seq % Q_BLOCK == 0 and seq % INPROJ_BLOCK == 0 and (batch * seq) % MERGE_BLOCK == 0
    assert (dec_batch * dec_seq) % INPROJ_BLOCK == 0 and (dec_batch * dec_seq) % MERGE_BLOCK == 0
    assert INPROJ_BLOCK % dec_seq == 0 and past % GM_CHUNK == 0 and (batch * n_mem) % TOKEN_BLOCK == 0
    assert past % CHUNK == 0 and dec_seq <= CHUNK, "new tokens must sit in one attention chunk"

    w = w_in[0]
    o = 0
    parts = []
    for width in (d, d, q_lora, kv_lora, ROPE_DIM, d_mem, 3 * d):
        parts.append(w[:, o:o + width])
        o += width
    w_u, w_v, w_cq, w_ckv, w_kr, w_qm, w_gate = parts
    w_c = jnp.concatenate([w_cq, w_kr, _swap_halves(w_kr)], axis=1)
    uq = mla_w_uq[0]
    uq_rope = uq[:, :, NOPE_DIM:]
    w_q3 = jnp.concatenate([uq[:, :, :NOPE_DIM].reshape(q_lora, -1),
                            uq_rope.reshape(q_lora, -1),
                            _swap_halves(uq_rope).reshape(q_lora, -1)], axis=1)
    w_uk = mla_w_uk[0].reshape(kv_lora, -1).astype(BF16)
    w_uv = mla_w_uv[0].reshape(kv_lora, -1).astype(BF16)
    in_weights = [x.astype(BF16) for x in (w_u, w_v, w_c, w_ckv, w_qm, w_gate, w_q3)]
    in_gains = [norm_mix_g[0][None], gm_norm_g[0][None], mla_q_norm_g[0][None], mla_kv_norm_g[0][None]]
    out_weights = [x[0].astype(BF16) for x in (w_br_gm, w_br_mla, w_br_mem, w_out, ffn_w_gate, ffn_w_up, ffn_w_down)]
    gw = gm_ws[0]
    gb = jnp.broadcast_to(gm_bs[0][:, :, None], gm_bs.shape[1:] + (LANE,))

    mk, mv, mkb, mvb = _memkv(mem_prompt.reshape(batch * n_mem, d), mem_norm_g[0][None], mem_w_kv[0].astype(BF16))
    xp = x_prompt.reshape(batch * seq, d)
    cos_p, sin_p = _rope_tables(jnp.arange(seq, dtype=jnp.int32))
    u, vn, q, ckv_p, kr_p, qm, gates, kn, vt, krp = _inproj(xp, cos_p, sin_p, in_gains, in_weights, (w_uk, w_uv.T))
    o_cat = _mix_prompt(batch, seq, u, vn, q, qm, kn, vt, krp, mkb, mvb, gw, gb)
    y_prompt = _merge_ffn(xp, o_cat, gates, norm_ffn_g[0][None], final_norm_g[None], out_weights)

    xs = x_sample.reshape(dec_batch * dec_seq, d)
    cos_s, sin_s = _rope_tables(past + jnp.arange(dec_seq, dtype=jnp.int32))
    reps = max(INPROJ_BLOCK // dec_seq, 1)
    cos_s, sin_s = jnp.tile(cos_s, (reps, 1)), jnp.tile(sin_s, (reps, 1))
    u, vn, q, ckv_s, kr_s, qm, gates, gv_s = _inproj(xs, cos_s, sin_s, in_gains, in_weights)
    cmk = cache_mem_k[0].reshape(dec_batch * n_mem, d_mem).astype(BF16)
    cmv = cache_mem_v[0].reshape(dec_batch * n_mem, d_mem).astype(BF16)
    o_cat = _mix_sample(dec_batch, dec_seq, past, u, vn, q, qm,
                        cache_mla_ckv[0].reshape(dec_batch * past, kv_lora),
                        cache_mla_krope[0].reshape(dec_batch * past, ROPE_DIM), ckv_s, kr_s, w_uk, w_uv,
                        cmk, cmv, gw, gb)
    y_sample = _merge_ffn(xs, o_cat, gates, norm_ffn_g[0][None], final_norm_g[None], out_weights)

    return (y_prompt.reshape(batch, seq, d), y_sample.reshape(dec_batch, dec_seq, d),
            ckv_p.reshape(1, batch, seq, kv_lora), kr_p.reshape(1, batch, seq, ROPE_DIM),
            mk.reshape(1, batch, n_mem, MEM_HEADS, MEM_HEAD_DIM), mv.reshape(1, batch, n_mem, MEM_HEADS, MEM_HEAD_DIM),
            ckv_s.reshape(1, dec_batch, dec_seq, kv_lora), kr_s.reshape(1, dec_batch, dec_seq, ROPE_DIM),
            gv_s.reshape(1, dec_batch, dec_seq, d))
```

```python
import functools
import math

import jax
import jax.numpy as jnp
from jax import lax
from jax.experimental import pallas as pl
from jax.experimental.pallas import tpu as pltpu

F32 = jnp.float32
BF16 = jnp.bfloat16

NORM_EPS = 1e-6
CHUNK = 64
GM_CHUNK = 128
GM_GROUPS = 8
MLA_HEADS = 8
NOPE_DIM = 128
ROPE_DIM = 64
V_DIM = 128
ROPE_THETA = 10000.0
MEM_HEADS = 4
MEM_HEAD_DIM = 256
LOG2E = math.log2(math.e)
MLA_QSCALE = (NOPE_DIM + ROPE_DIM) ** -0.5 * LOG2E
MEM_QSCALE = MEM_HEAD_DIM ** -0.5 * LOG2E

LANE = 128
HEAD_PAD = 2 * LANE
SUM_ROWS = 16
SOFTMAX_LAG = 2
VALUES_LAG = 4
STAGE_SLOTS = VALUES_LAG + 1
STRIP = 32
VMEM_LIMIT_BYTES = 56 * 1024 * 1024

TOKEN_BLOCK = 512
INPROJ_BLOCK = 512
INPROJ_GROUPS = 2
MERGE_BLOCK = 512
MERGE_GROUPS = 2
Q_BLOCK = 512
KEY_SLAB = 256


def _params(n_axes):
    return pltpu.CompilerParams(dimension_semantics=("arbitrary",) * n_axes,
                                vmem_limit_bytes=VMEM_LIMIT_BYTES)


def _resident(shape):
    return pl.BlockSpec(shape, lambda *_: (0,) * len(shape), pipeline_mode=pl.Buffered(1))


def _rows(tm, width):
    return pl.BlockSpec((tm, width), lambda i: (i, 0))


def _dot(a, b):
    return jnp.dot(a, b, preferred_element_type=F32)


def _dot_nt(a, b):
    return lax.dot_general(a, b, (((1,), (1,)), ((), ())), preferred_element_type=F32)


def _rms(x, g):
    return x * lax.rsqrt(jnp.mean(x * x, axis=-1, keepdims=True) + NORM_EPS) * g


def _memkv_kernel(mem_ref, g_ref, w_ref, k_ref, v_ref, kb_ref, vb_ref):
    d = kb_ref.shape[1]
    kv = _dot(_rms(mem_ref[...], g_ref[...]).astype(BF16), w_ref[...])
    k, v = kv[:, :d], kv[:, d:]
    for h in range(MEM_HEADS):
        k_ref[:, h, :] = k[:, h * MEM_HEAD_DIM:(h + 1) * MEM_HEAD_DIM]
        v_ref[:, h, :] = v[:, h * MEM_HEAD_DIM:(h + 1) * MEM_HEAD_DIM]
    kb_ref[...] = k.astype(BF16)
    vb_ref[...] = v.astype(BF16)


def _memkv(mem2d, g, w_kv):
    n, d = mem2d.shape
    dm = w_kv.shape[1] // 2
    tm = TOKEN_BLOCK
    return pl.pallas_call(
        _memkv_kernel,
        grid=(n // tm,),
        in_specs=[_rows(tm, d), _resident((1, d)), _resident(w_kv.shape)],
        out_specs=[pl.BlockSpec((tm, MEM_HEADS, MEM_HEAD_DIM), lambda i: (i, 0, 0))] * 2 + [_rows(tm, dm)] * 2,
        out_shape=([jax.ShapeDtypeStruct((n, MEM_HEADS, MEM_HEAD_DIM), F32)] * 2
                   + [jax.ShapeDtypeStruct((n, dm), BF16)] * 2),
        compiler_params=_params(1),
        name="memkv",
    )(mem2d, g, w_kv)


def _inproj_kernel(x_ref, cos_ref, sin_ref, gmix_ref, ggm_ref, gq_ref, gkv_ref,
                   wu_ref, wv_ref, wc_ref, wckv_ref, wqm_ref, wg_ref, wq3_ref, *rest, prompt):
    if prompt:
        wuk_ref, wuvt_ref, u_ref, vn_ref, q_ref, ckv_ref, kr_ref, qm_ref, gate_ref, kn_ref, vt_ref, krp_ref = rest
    else:
        u_ref, vn_ref, q_ref, ckv_ref, kr_ref, qm_ref, gate_ref, vn32_ref = rest
    tm, d = x_ref.shape
    q_lora = wq3_ref.shape[0]
    nn, nr = MLA_HEADS * NOPE_DIM, MLA_HEADS * ROPE_DIM
    low = lax.broadcasted_iota(jnp.int32, (1, LANE), 1) < ROPE_DIM

    def row_group(r):
        xn = _rms(x_ref[r, :], gmix_ref[...]).astype(BF16)
        cos = cos_ref[r, :]
        sin = sin_ref[r, :]

        def gate_chunk(c):
            cols = slice(c * d, (c + 1) * d)
            gate_ref[r, cols] = jax.nn.sigmoid(_dot(xn, wg_ref[:, cols])).astype(BF16)

        zc = _dot(xn, wc_ref[...])
        zkv = _dot(xn, wckv_ref[...])
        zu = _dot(xn, wu_ref[...])
        yield

        cqn = _rms(zc[:, :q_lora], gq_ref[...]).astype(BF16)
        ckv = _rms(zkv, gkv_ref[...])
        ckv_ref[r, :] = ckv
        krk = zc[:, q_lora:]
        krr = krk * jnp.where(low, cos, 0.0) + pltpu.roll(krk, ROPE_DIM, 1) * jnp.where(low, sin, 0.0)
        kr_ref[r, :] = krr[:, :ROPE_DIM]
        q3 = _dot(cqn, wq3_ref[...])
        gate_chunk(0)
        yield

        u_ref[r, :] = jax.nn.gelu(zu).astype(BF16)
        zv = _dot(xn, wv_ref[...])
        if prompt:
            ckvb = ckv.astype(BF16)
            kn_ref[r, :] = _dot(ckvb, wuk_ref[...]).astype(BF16)
            step = min(KEY_SLAB, ckvb.shape[0])
            for lo in range(0, ckvb.shape[0], step):
                k0 = r.start + lo
                vt_ref[k0 // KEY_SLAB, :, k0 % KEY_SLAB:k0 % KEY_SLAB + step] = _dot_nt(
                    wuvt_ref[...], ckvb[lo:lo + step, :]).astype(BF16)
            krp_ref[r, :] = krr.astype(BF16)
        yield

        zeros = jnp.zeros((q3.shape[0], LANE - ROPE_DIM), F32)
        for pair in range(MLA_HEADS // 2):
            lo = pair * LANE
            rot = (q3[:, nn + lo:nn + lo + LANE] * cos + q3[:, nn + nr + lo:nn + nr + lo + LANE] * sin) * MLA_QSCALE
            for h, piece in ((2 * pair, rot[:, :ROPE_DIM]), (2 * pair + 1, rot[:, ROPE_DIM:])):
                q_ref[r, h * HEAD_PAD:h * HEAD_PAD + LANE] = (q3[:, h * LANE:(h + 1) * LANE]
                                                              * MLA_QSCALE).astype(BF16)
                q_ref[r, h * HEAD_PAD + LANE:(h + 1) * HEAD_PAD] = jnp.concatenate([piece, zeros],
                                                                                   axis=1).astype(BF16)
        gate_chunk(1)
        yield

        vn = _rms(jax.nn.gelu(zv), ggm_ref[...])
        vn_ref[r, :] = vn.astype(BF16)
        if not prompt:
            vn32_ref[r, :] = vn
        yield

        qm_ref[r, :] = (_dot(xn, wqm_ref[...]) * MEM_QSCALE).astype(BF16)
        gate_chunk(2)

    rows = tm // INPROJ_GROUPS
    pending = []
    for g in range(INPROJ_GROUPS):
        pending.append(row_group(slice(g * rows, (g + 1) * rows)))
        for gen in list(pending):
            if next(gen, "done") == "done":
                pending.remove(gen)
    while pending:
        for gen in list(pending):
            if next(gen, "done") == "done":
                pending.remove(gen)


def _inproj(x2d, cos_tab, sin_tab, gains, weights, kv_weights=None):
    n, d = x2d.shape
    tm = INPROJ_BLOCK
    prompt = kv_weights is not None
    tab_blocks = cos_tab.shape[0] // tm
    tab_spec = pl.BlockSpec((tm, LANE), lambda i: (i % tab_blocks, 0))
    outs = [(d, BF16), (d, BF16), (MLA_HEADS * HEAD_PAD, BF16), (weights[3].shape[1], F32), (ROPE_DIM, F32),
            (MEM_HEADS * MEM_HEAD_DIM, BF16), (3 * d, BF16)]
    outs += [(MLA_HEADS * NOPE_DIM, BF16), None, (LANE, BF16)] if prompt else [(d, F32)]
    out_specs = [w and _rows(tm, w[0]) for w in outs]
    out_shape = [w and jax.ShapeDtypeStruct((n, w[0]), w[1]) for w in outs]
    if prompt:
        hv = MLA_HEADS * V_DIM
        out_specs[8] = pl.BlockSpec((tm // KEY_SLAB, hv, KEY_SLAB), lambda i: (i, 0, 0))
        out_shape[8] = jax.ShapeDtypeStruct((n // KEY_SLAB, hv, KEY_SLAB), BF16)
        weights = list(weights) + list(kv_weights)
    return pl.pallas_call(
        functools.partial(_inproj_kernel, prompt=prompt),
        grid=(n // tm,),
        in_specs=([_rows(tm, d), tab_spec, tab_spec] + [_resident(g.shape) for g in gains]
                  + [_resident(w.shape) for w in weights]),
        out_specs=out_specs,
        out_shape=out_shape,
        compiler_params=_params(1),
        name="inproj",
    )(x2d, cos_tab, sin_tab, *gains, *weights)


def _tril_weights(gw_ref, g, rows):
    w = gw_ref[g][:rows, :rows]
    r = lax.broadcasted_iota(jnp.int32, (rows, rows), 0)
    c = lax.broadcasted_iota(jnp.int32, (rows, rows), 1)
    return jnp.where(c <= r, w, 0.0).astype(BF16)


def _mem_attend(qm_ref, mk_ref, mv_ref, o_ref, col0):
    for h in range(MEM_HEADS):
        sl = slice(h * MEM_HEAD_DIM, (h + 1) * MEM_HEAD_DIM)
        s = _dot_nt(qm_ref[:, sl], mk_ref[:, sl])
        p = jnp.exp2(s - jnp.max(s, axis=1, keepdims=True))
        o = _dot(p.astype(BF16), mv_ref[:, sl]) / jnp.sum(p, axis=1, keepdims=True)
        o_ref[:, col0 + h * MEM_HEAD_DIM:col0 + (h + 1) * MEM_HEAD_DIM] = o.astype(BF16)


def _mix_prompt_kernel(u_ref, vn_ref, q_ref, qm_ref, kn_ref, vt_ref, krp_ref, mk_ref, mv_ref, gw_ref, gb_ref, o_ref,
                       m_sc, acc_sc, *stage_sc):
    st_sc, p_sc = stage_sc[:STAGE_SLOTS], stage_sc[STAGE_SLOTS:]
    qi = pl.program_id(1)
    tq, d = u_ref.shape

    for g in range(GM_GROUPS):
        wg = _tril_weights(gw_ref, g, GM_CHUNK)
        bias = gb_ref[g]
        cs = slice(g * LANE, (g + 1) * LANE)
        for r0 in range(0, tq, 2 * GM_CHUNK):
            ra = slice(r0, r0 + GM_CHUNK)
            rb = slice(r0 + GM_CHUNK, r0 + 2 * GM_CHUNK)
            mixed = _dot(wg, jnp.concatenate([vn_ref[ra, cs], vn_ref[rb, cs]], axis=1))
            o_ref[ra, cs] = (u_ref[ra, cs].astype(F32) * (mixed[:, :LANE] + bias)).astype(BF16)
            o_ref[rb, cs] = (u_ref[rb, cs].astype(F32) * (mixed[:, LANE:] + bias)).astype(BF16)

    half = tq // 2
    kc = lax.broadcasted_iota(jnp.int32, (half, half), 0) // CHUNK
    qc = lax.broadcasted_iota(jnp.int32, (half, half), 1) // CHUNK
    visible = kc <= qc

    ones_rows = jnp.ones((SUM_ROWS, tq), BF16)

    def key_blocks(js, first=False):
        items = [(b, h) for b in range(len(js)) for h in range(MLA_HEADS)]
        scores, probs = {}, {}

        def stage_scores(item):
            b, h = item
            rows = pl.ds(pl.multiple_of(js[b] * tq, tq), tq)
            kh = jnp.concatenate([kn_ref[rows, h * LANE:(h + 1) * LANE], krp_ref[rows, :]], axis=1)
            qh = q_ref[:, h * HEAD_PAD:(h + 1) * HEAD_PAD]
            if first:
                scores[item] = (_dot_nt(kh[:half], qh[:half]), _dot_nt(kh, qh[half:]))
            else:
                st_sc[items.index(item) % STAGE_SLOTS][0] = _dot_nt(kh, qh)

        def stage_softmax(item):
            h = item[1]
            if first:
                s0, s1 = scores.pop(item)
                s0 = jnp.where(visible, s0, -jnp.inf)
                s1 = jnp.concatenate([s1[:half], jnp.where(visible, s1[half:], -jnp.inf)], axis=0)
                m0 = jnp.max(s0, axis=0, keepdims=True)
                m1 = jnp.max(s1, axis=0, keepdims=True)
                m_sc[h] = jnp.concatenate([m0, m1], axis=1)
                probs[item] = (jnp.exp2(s0 - m0).astype(BF16), jnp.exp2(s1 - m1).astype(BF16))
            else:
                slot = items.index(item) % STAGE_SLOTS
                zero = jnp.minimum(js[0], 0)
                strips = [slice(r0, r0 + STRIP) for r0 in range(0, tq, STRIP)]
                m8 = None
                for rs in strips:
                    part = jnp.max(st_sc[slot][zero, rs, :].reshape(STRIP // 8, 8, tq), axis=0)
                    m8 = part if m8 is None else jnp.maximum(m8, part)
                m_old = m_sc[h]
                m = jnp.maximum(m_old, jnp.max(m8, axis=0, keepdims=True))
                m_sc[h] = m
                for rs in strips:
                    p_sc[slot][0, rs, :] = jnp.exp2(st_sc[slot][zero, rs, :] - m).astype(BF16)
                probs[item] = jnp.exp2(m_old - m)

        def stage_values(item):
            j, h = js[item[0]], item[1]
            slabs = tq // KEY_SLAB
            vt = jnp.concatenate([vt_ref[j * slabs + i, h * V_DIM:(h + 1) * V_DIM, :] for i in range(slabs)], axis=1)
            vt = jnp.concatenate([vt, ones_rows], axis=0)
            if first:
                p0, p1 = probs.pop(item)
                acc_sc[h] = jnp.concatenate([_dot(vt[:, :half], p0), _dot(vt, p1)], axis=1)
            else:
                p = p_sc[items.index(item) % STAGE_SLOTS][jnp.minimum(js[0], 0)]
                acc_sc[h] = probs.pop(item) * acc_sc[h] + _dot(vt, p)

        for step in range(len(items) + VALUES_LAG):
            if step < len(items):
                stage_scores(items[step])
            if 0 <= step - SOFTMAX_LAG < len(items):
                stage_softmax(items[step - SOFTMAX_LAG])
            if 0 <= step - VALUES_LAG < len(items):
                stage_values(items[step - VALUES_LAG])

    key_blocks([qi], first=True)

    def past_block(j, carry):
        key_blocks([j])
        return carry

    lax.fori_loop(0, qi, past_block, 0)
    for h in range(MLA_HEADS):
        acc = acc_sc[h]
        o = acc[:V_DIM, :] * (1.0 / acc[V_DIM:V_DIM + 1, :])
        o_ref[:, d + h * V_DIM:d + (h + 1) * V_DIM] = o.T.astype(BF16)

    _mem_attend(qm_ref, mk_ref, mv_ref, o_ref, d + MLA_HEADS * V_DIM)


def _mix_prompt(batch, seq, u, vn, q, qm, kn, vt, krp, mkb, mvb, gw, gb):
    n, d = u.shape
    tq = Q_BLOCK
    nq = seq // tq
    n_mem = mkb.shape[0] // batch
    blk = lambda w: pl.BlockSpec((tq, w), lambda b, i: (b * nq + i, 0))
    per_batch = lambda rows, w: pl.BlockSpec((rows, w), lambda b, i: (b, 0))
    return pl.pallas_call(
        _mix_prompt_kernel,
        grid=(batch, nq),
        in_specs=[blk(d), blk(d), blk(q.shape[1]), blk(qm.shape[1]),
                  per_batch(seq, kn.shape[1]),
                  pl.BlockSpec((seq // KEY_SLAB,) + vt.shape[1:], lambda b, i: (b, 0, 0)),
                  per_batch(seq, LANE),
                  per_batch(n_mem, mkb.shape[1]), per_batch(n_mem, mvb.shape[1]),
                  _resident(gw.shape), _resident(gb.shape)],
        out_specs=blk(3 * d),
        out_shape=jax.ShapeDtypeStruct((n, 3 * d), BF16),
        scratch_shapes=[pltpu.VMEM((MLA_HEADS, 1, tq), F32), pltpu.VMEM((MLA_HEADS, V_DIM + SUM_ROWS, tq), F32),
                        *[pltpu.VMEM((1, tq, tq), F32)] * STAGE_SLOTS, *[pltpu.VMEM((1, tq, tq), BF16)] * STAGE_SLOTS],
        compiler_params=_params(2),
        name="mix_prompt",
    )(u, vn, q, qm, kn, vt, krp, mkb, mvb, gw, gb)


def _mix_sample_kernel(u_ref, vn_ref, q_ref, qm_ref, ckvp_ref, krp_ref, ckvn_ref, krn_ref, wuk_ref, wuv_ref,
                       mk_ref, mv_ref, gw_ref, gb_ref, o_ref):
    ts, d = u_ref.shape

    for g in range(GM_GROUPS):
        cs = slice(g * LANE, (g + 1) * LANE)
        mixed = _dot(_tril_weights(gw_ref, g, ts), vn_ref[:, cs]) + gb_ref[g][:ts, :]
        o_ref[:, cs] = (u_ref[:, cs].astype(F32) * mixed).astype(BF16)

    def keys(c_ref, r_ref):
        r = r_ref[...]
        c = c_ref[...].astype(BF16)
        return c, jnp.concatenate([c, jnp.concatenate([r, jnp.zeros_like(r)], axis=1).astype(BF16)], axis=1)

    cp, kp = keys(ckvp_ref, krp_ref)
    cn, kn = keys(ckvn_ref, krn_ref)
    qs = []
    for h in range(MLA_HEADS):
        q_abs = _dot_nt(q_ref[:, h * HEAD_PAD:h * HEAD_PAD + NOPE_DIM], wuk_ref[:, h * NOPE_DIM:(h + 1) * NOPE_DIM])
        qs.append(jnp.concatenate([q_abs.astype(BF16), q_ref[:, h * HEAD_PAD + NOPE_DIM:(h + 1) * HEAD_PAD]], axis=1))
    qa = jnp.concatenate(qs, axis=0)
    s_past = _dot_nt(qa, kp)
    s_new = _dot_nt(qa, kn)
    m = jnp.maximum(jnp.max(s_past, axis=1, keepdims=True), jnp.max(s_new, axis=1, keepdims=True))
    p_past = jnp.exp2(s_past - m)
    p_new = jnp.exp2(s_new - m)
    l = jnp.sum(p_past, axis=1, keepdims=True) + jnp.sum(p_new, axis=1, keepdims=True)
    lat = ((_dot(p_past.astype(BF16), cp) + _dot(p_new.astype(BF16), cn)) / l).astype(BF16)
    for h in range(MLA_HEADS):
        o_ref[:, d + h * V_DIM:d + (h + 1) * V_DIM] = _dot(lat[h * ts:(h + 1) * ts, :],
                                                          wuv_ref[:, h * V_DIM:(h + 1) * V_DIM]).astype(BF16)

    _mem_attend(qm_ref, mk_ref, mv_ref, o_ref, d + MLA_HEADS * V_DIM)


def _mix_sample(batch, ts, past, u, vn, q, qm, ckv_past, kr_past, ckv_new, kr_new, w_uk, w_uv, mkb, mvb, gw, gb):
    n, d = u.shape
    n_mem = mkb.shape[0] // batch
    per_batch = lambda rows, w: pl.BlockSpec((rows, w), lambda b: (b, 0))
    return pl.pallas_call(
        _mix_sample_kernel,
        grid=(batch,),
        in_specs=[per_batch(ts, d), per_batch(ts, d), per_batch(ts, q.shape[1]), per_batch(ts, qm.shape[1]),
                  per_batch(past, ckv_past.shape[1]), per_batch(past, ROPE_DIM),
                  per_batch(ts, ckv_new.shape[1]), per_batch(ts, ROPE_DIM),
                  _resident(w_uk.shape), _resident(w_uv.shape),
                  per_batch(n_mem, mkb.shape[1]), per_batch(n_mem, mvb.shape[1]),
                  _resident(gw.shape), _resident(gb.shape)],
        out_specs=per_batch(ts, 3 * d),
        out_shape=jax.ShapeDtypeStruct((n, 3 * d), BF16),
        compiler_params=_params(1),
        name="mix_sample",
    )(u, vn, q, qm, ckv_past, kr_past, ckv_new, kr_new, w_uk, w_uv, mkb, mvb, gw, gb)


def _merge_ffn_kernel(x_ref, o_ref, gate_ref, gffn_ref, gfin_ref, wgm_ref, wmla_ref, wmem_ref, wout_ref,
                      wfg_ref, wfu_ref, wfd_ref, y_ref):
    tm, d = x_ref.shape

    def merge(r):
        merged = (gate_ref[r, 0:d].astype(F32) * _dot(o_ref[r, 0:d], wgm_ref[...])
                  + gate_ref[r, d:2 * d].astype(F32) * _dot(o_ref[r, d:2 * d], wmla_ref[...])
                  + gate_ref[r, 2 * d:3 * d].astype(F32) * _dot(o_ref[r, 2 * d:3 * d], wmem_ref[...]))
        return x_ref[r, :] + _dot(merged.astype(BF16), wout_ref[...])

    def ffn_hidden(h):
        hn = _rms(h, gffn_ref[...]).astype(BF16)
        return (jax.nn.silu(_dot(hn, wfg_ref[...])) * _dot(hn, wfu_ref[...])).astype(BF16)

    def finish(r, h, a):
        y_ref[r, :] = _rms(h + _dot(a, wfd_ref[...]), gfin_ref[...])

    groups = [slice(i * tm // MERGE_GROUPS, (i + 1) * tm // MERGE_GROUPS) for i in range(MERGE_GROUPS)]
    hs = [merge(r) for r in groups]
    acts = [ffn_hidden(h) for h in hs]
    for r, h, a in zip(groups, hs, acts):
        finish(r, h, a)


def _merge_ffn(x2d, o_cat, gates, g_ffn, g_fin, weights):
    n, d = x2d.shape
    tm = MERGE_BLOCK
    return pl.pallas_call(
        _merge_ffn_kernel,
        grid=(n // tm,),
        in_specs=([_rows(tm, d), _rows(tm, 3 * d), _rows(tm, 3 * d), _resident(g_ffn.shape), _resident(g_fin.shape)]
                  + [_resident(w.shape) for w in weights]),
        out_specs=_rows(tm, d),
        out_shape=jax.ShapeDtypeStruct((n, d), F32),
        compiler_params=_params(1),
        name="merge_ffn",
    )(x2d, o_cat, gates, g_ffn, g_fin, *weights)


def _rope_tables(pos):
    inv = ROPE_THETA ** (-jnp.arange(0, ROPE_DIM, 2, dtype=F32) / ROPE_DIM)
    ang = pos.astype(F32)[:, None] * inv[None, :]
    cos, sin = jnp.cos(ang), jnp.sin(ang)
    return jnp.concatenate([cos, cos, cos, cos], axis=1), jnp.concatenate([-sin, sin, -sin, sin], axis=1)


def _swap_halves(w):
    half = w.shape[-1] // 2
    return jnp.concatenate([w[..., half:], w[..., :half]], axis=-1)


def kernel(x_prompt, x_sample, cache_mla_ckv, cache_mla_krope, cache_mem_k, cache_mem_v, mem_prompt, norm_mix_g, w_in, gm_norm_g, gm_ws, gm_bs, mla_q_norm_g, mla_w_uq, mla_kv_norm_g, mla_w_uk, mla_w_uv, mem_norm_g, mem_w_kv, w_br_gm, w_br_mla, w_br_mem, w_out, norm_ffn_g, ffn_w_gate, ffn_w_up, ffn_w_down, final_norm_g):
    depth = w_in.shape[0]
    assert depth == 1, "single-layer step"
    batch, seq, d = x_prompt.shape
    dec_batch, dec_seq, _ = x_sample.shape
    past = cache_mla_ckv.shape[2]
    n_mem = mem_prompt.shape[1]
    q_lora = mla_w_uq.shape[1]
    kv_lora = mla_w_uk.shape[1]
    d_mem = MEM_HEADS * MEM_HEAD_DIM
    assert seq % Q_BLOCK == 0 and seq % INPROJ_BLOCK == 0 and (batch * seq) % MERGE_BLOCK == 0
    assert (dec_batch * dec_seq) % INPROJ_BLOCK == 0 and (dec_batch * dec_seq) % MERGE_BLOCK == 0
    assert INPROJ_BLOCK % dec_seq == 0 and past % GM_CHUNK == 0 and (batch * n_mem) % TOKEN_BLOCK == 0
    assert past % CHUNK == 0 and dec_seq <= CHUNK, "new tokens must sit in one attention chunk"

    w = w_in[0]
    o = 0
    parts = []
    for width in (d, d, q_lora, kv_lora, ROPE_DIM, d_mem, 3 * d):
        parts.append(w[:, o:o + width])
        o += width
    w_u, w_v, w_cq, w_ckv, w_kr, w_qm, w_gate = parts
    w_c = jnp.concatenate([w_cq, w_kr, _swap_halves(w_kr)], axis=1)
    uq = mla_w_uq[0]
    uq_rope = uq[:, :, NOPE_DIM:]
    w_q3 = jnp.concatenate([uq[:, :, :NOPE_DIM].reshape(q_lora, -1),
                            uq_rope.reshape(q_lora, -1),
                            _swap_halves(uq_rope).reshape(q_lora, -1)], axis=1)
    w_uk = mla_w_uk[0].reshape(kv_lora, -1).astype(BF16)
    w_uv = mla_w_uv[0].reshape(kv_lora, -1).astype(BF16)
    in_weights = [x.astype(BF16) for x in (w_u, w_v, w_c, w_ckv, w_qm, w_gate, w_q3)]
    in_gains = [norm_mix_g[0][None], gm_norm_g[0][None], mla_q_norm_g[0][None], mla_kv_norm_g[0][None]]
    out_weights = [x[0].astype(BF16) for x in (w_br_gm, w_br_mla, w_br_mem, w_out, ffn_w_gate, ffn_w_up, ffn_w_down)]
    gw = gm_ws[0]
    gb = jnp.broadcast_to(gm_bs[0][:, :, None], gm_bs.shape[1:] + (LANE,))

    mk, mv, mkb, mvb = _memkv(mem_prompt.reshape(batch * n_mem, d), mem_norm_g[0][None], mem_w_kv[0].astype(BF16))
    xp = x_prompt.reshape(batch * seq, d)
    cos_p, sin_p = _rope_tables(jnp.arange(seq, dtype=jnp.int32))
    u, vn, q, ckv_p, kr_p, qm, gates, kn, vt, krp = _inproj(xp, cos_p, sin_p, in_gains, in_weights, (w_uk, w_uv.T))
    o_cat = _mix_prompt(batch, seq, u, vn, q, qm, kn, vt, krp, mkb, mvb, gw, gb)
    y_prompt = _merge_ffn(xp, o_cat, gates, norm_ffn_g[0][None], final_norm_g[None], out_weights)

    xs = x_sample.reshape(dec_batch * dec_seq, d)
    cos_s, sin_s = _rope_tables(past + jnp.arange(dec_seq, dtype=jnp.int32))
    reps = max(INPROJ_BLOCK // dec_seq, 1)
    cos_s, sin_s = jnp.tile(cos_s, (reps, 1)), jnp.tile(sin_s, (reps, 1))
    u, vn, q, ckv_s, kr_s, qm, gates, gv_s = _inproj(xs, cos_s, sin_s, in_gains, in_weights)
    cmk = cache_mem_k[0].reshape(dec_batch * n_mem, d_mem).astype(BF16)
    cmv = cache_mem_v[0].reshape(dec_batch * n_mem, d_mem).astype(BF16)
    o_cat = _mix_sample(dec_batch, dec_seq, past, u, vn, q, qm,
                        cache_mla_ckv[0].reshape(dec_batch * past, kv_lora),
                        cache_mla_krope[0].reshape(dec_batch * past, ROPE_DIM), ckv_s, kr_s, w_uk, w_uv,
                        cmk, cmv, gw, gb)
    y_sample = _merge_ffn(xs, o_cat, gates, norm_ffn_g[0][None], final_norm_g[None], out_weights)

    return (y_prompt.reshape(batch, seq, d), y_sample.reshape(dec_batch, dec_seq, d),
            ckv_p.reshape(1, batch, seq, kv_lora), kr_p.reshape(1, batch, seq, ROPE_DIM),
            mk.reshape(1, batch, n_mem, MEM_HEADS, MEM_HEAD_DIM), mv.reshape(1, batch, n_mem, MEM_HEADS, MEM_HEAD_DIM),
            ckv_s.reshape(1, dec_batch, dec_seq, kv_lora), kr_s.reshape(1, dec_batch, dec_seq, ROPE_DIM),
            gv_s.reshape(1, dec_batch, dec_seq, d))
```

```python
import functools
import math

import jax
import jax.numpy as jnp
from jax import lax
from jax.experimental import pallas as pl
from jax.experimental.pallas import tpu as pltpu

F32 = jnp.float32
BF16 = jnp.bfloat16

NORM_EPS = 1e-6
CHUNK = 64
GM_CHUNK = 128
GM_GROUPS = 8
MLA_HEADS = 8
NOPE_DIM = 128
ROPE_DIM = 64
V_DIM = 128
ROPE_THETA = 10000.0
MEM_HEADS = 4
MEM_HEAD_DIM = 256
LOG2E = math.log2(math.e)
MLA_QSCALE = (NOPE_DIM + ROPE_DIM) ** -0.5 * LOG2E
MEM_QSCALE = MEM_HEAD_DIM ** -0.5 * LOG2E

LANE = 128
HEAD_PAD = 2 * LANE
SUM_ROWS = 16
SOFTMAX_LAG = 2
VALUES_LAG = 4
VMEM_LIMIT_BYTES = 56 * 1024 * 1024

TOKEN_BLOCK = 512
INPROJ_BLOCK = 512
INPROJ_GROUPS = 2
MERGE_BLOCK = 512
MERGE_GROUPS = 2
Q_BLOCK = 512
KEY_SLAB = 256


def _params(n_axes):
    return pltpu.CompilerParams(dimension_semantics=("arbitrary",) * n_axes,
                                vmem_limit_bytes=VMEM_LIMIT_BYTES)


def _resident(shape):
    return pl.BlockSpec(shape, lambda *_: (0,) * len(shape), pipeline_mode=pl.Buffered(1))


def _rows(tm, width):
    return pl.BlockSpec((tm, width), lambda i: (i, 0))


def _dot(a, b):
    return jnp.dot(a, b, preferred_element_type=F32)


def _dot_nt(a, b):
    return lax.dot_general(a, b, (((1,), (1,)), ((), ())), preferred_element_type=F32)


def _rms(x, g):
    return x * lax.rsqrt(jnp.mean(x * x, axis=-1, keepdims=True) + NORM_EPS) * g


def _memkv_kernel(mem_ref, g_ref, w_ref, k_ref, v_ref, kb_ref, vb_ref):
    d = kb_ref.shape[1]
    kv = _dot(_rms(mem_ref[...], g_ref[...]).astype(BF16), w_ref[...])
    k, v = kv[:, :d], kv[:, d:]
    for h in range(MEM_HEADS):
        k_ref[:, h, :] = k[:, h * MEM_HEAD_DIM:(h + 1) * MEM_HEAD_DIM]
        v_ref[:, h, :] = v[:, h * MEM_HEAD_DIM:(h + 1) * MEM_HEAD_DIM]
    kb_ref[...] = k.astype(BF16)
    vb_ref[...] = v.astype(BF16)


def _memkv(mem2d, g, w_kv):
    n, d = mem2d.shape
    dm = w_kv.shape[1] // 2
    tm = TOKEN_BLOCK
    return pl.pallas_call(
        _memkv_kernel,
        grid=(n // tm,),
        in_specs=[_rows(tm, d), _resident((1, d)), _resident(w_kv.shape)],
        out_specs=[pl.BlockSpec((tm, MEM_HEADS, MEM_HEAD_DIM), lambda i: (i, 0, 0))] * 2 + [_rows(tm, dm)] * 2,
        out_shape=([jax.ShapeDtypeStruct((n, MEM_HEADS, MEM_HEAD_DIM), F32)] * 2
                   + [jax.ShapeDtypeStruct((n, dm), BF16)] * 2),
        compiler_params=_params(1),
        name="memkv",
    )(mem2d, g, w_kv)


def _inproj_kernel(x_ref, cos_ref, sin_ref, gmix_ref, ggm_ref, gq_ref, gkv_ref,
                   wu_ref, wv_ref, wc_ref, wckv_ref, wqm_ref, wg_ref, wq3_ref, *rest, prompt):
    if prompt:
        (wuk_ref, wuvt_ref, mk_ref, mv_ref, gw_ref, gb_ref,
         ogm_ref, omem_ref, q_ref, ckv_ref, kr_ref, gate_ref, kn_ref, vt_ref, krp_ref, u_ref) = rest
    else:
        u_ref, vn_ref, q_ref, ckv_ref, kr_ref, qm_ref, gate_ref, vn32_ref = rest
    tm, d = x_ref.shape
    q_lora = wq3_ref.shape[0]
    nn, nr = MLA_HEADS * NOPE_DIM, MLA_HEADS * ROPE_DIM
    low = lax.broadcasted_iota(jnp.int32, (1, LANE), 1) < ROPE_DIM

    def row_group(r):
        xn = _rms(x_ref[r, :], gmix_ref[...]).astype(BF16)
        cos = cos_ref[r, :]
        sin = sin_ref[r, :]

        def gate_chunk(c):
            cols = slice(c * d, (c + 1) * d)
            gate_ref[r, cols] = jax.nn.sigmoid(_dot(xn, wg_ref[:, cols])).astype(BF16)

        zc = _dot(xn, wc_ref[...])
        zkv = _dot(xn, wckv_ref[...])
        zu = _dot(xn, wu_ref[...])
        yield

        cqn = _rms(zc[:, :q_lora], gq_ref[...]).astype(BF16)
        ckv = _rms(zkv, gkv_ref[...])
        ckv_ref[r, :] = ckv
        krk = zc[:, q_lora:]
        krr = krk * jnp.where(low, cos, 0.0) + pltpu.roll(krk, ROPE_DIM, 1) * jnp.where(low, sin, 0.0)
        kr_ref[r, :] = krr[:, :ROPE_DIM]
        q3 = _dot(cqn, wq3_ref[...])
        gate_chunk(0)
        yield

        u_ref[r, :] = jax.nn.gelu(zu).astype(BF16)
        zv = _dot(xn, wv_ref[...])
        if prompt:
            ckvb = ckv.astype(BF16)
            kn_ref[r, :] = _dot(ckvb, wuk_ref[...]).astype(BF16)
            step = min(KEY_SLAB, ckvb.shape[0])
            for lo in range(0, ckvb.shape[0], step):
                k0 = r.start + lo
                vt_ref[k0 // KEY_SLAB, :, k0 % KEY_SLAB:k0 % KEY_SLAB + step] = _dot_nt(
                    wuvt_ref[...], ckvb[lo:lo + step, :]).astype(BF16)
            krp_ref[r, :] = krr.astype(BF16)
        yield

        zeros = jnp.zeros((q3.shape[0], LANE - ROPE_DIM), F32)
        for pair in range(MLA_HEADS // 2):
            lo = pair * LANE
            rot = (q3[:, nn + lo:nn + lo + LANE] * cos + q3[:, nn + nr + lo:nn + nr + lo + LANE] * sin) * MLA_QSCALE
            for h, piece in ((2 * pair, rot[:, :ROPE_DIM]), (2 * pair + 1, rot[:, ROPE_DIM:])):
                q_ref[r, h * HEAD_PAD:h * HEAD_PAD + LANE] = (q3[:, h * LANE:(h + 1) * LANE]
                                                              * MLA_QSCALE).astype(BF16)
                q_ref[r, h * HEAD_PAD + LANE:(h + 1) * HEAD_PAD] = jnp.concatenate([piece, zeros],
                                                                                   axis=1).astype(BF16)
        gate_chunk(1)
        yield

        vn = _rms(jax.nn.gelu(zv), ggm_ref[...])
        if prompt:
            vnb = vn.astype(BF16)
            for g in range(GM_GROUPS):
                wg = _tril_weights(gw_ref, g, GM_CHUNK)
                bias = gb_ref[g]
                cs = slice(g * LANE, (g + 1) * LANE)
                for c0 in range(0, vnb.shape[0], 2 * GM_CHUNK):
                    mixed = _dot(wg, jnp.concatenate([vnb[c0:c0 + GM_CHUNK, cs],
                                                      vnb[c0 + GM_CHUNK:c0 + 2 * GM_CHUNK, cs]], axis=1))
                    for k in range(2):
                        rows = slice(r.start + c0 + k * GM_CHUNK, r.start + c0 + (k + 1) * GM_CHUNK)
                        ogm_ref[rows, cs] = (u_ref[rows, cs].astype(F32)
                                             * (mixed[:, k * LANE:(k + 1) * LANE] + bias)).astype(BF16)
        else:
            vn_ref[r, :] = vn.astype(BF16)
            vn32_ref[r, :] = vn
        yield

        qm = (_dot(xn, wqm_ref[...]) * MEM_QSCALE).astype(BF16)
        gate_chunk(2)
        if prompt:
            for h in range(MEM_HEADS):
                sl = slice(h * MEM_HEAD_DIM, (h + 1) * MEM_HEAD_DIM)
                sc = _dot_nt(qm[:, sl], mk_ref[:, sl])
                p = jnp.exp2(sc - jnp.max(sc, axis=1, keepdims=True))
                o = _dot(p.astype(BF16), mv_ref[:, sl]) / jnp.sum(p, axis=1, keepdims=True)
                omem_ref[r, sl] = o.astype(BF16)
        else:
            qm_ref[r, :] = qm

    rows = tm // INPROJ_GROUPS
    pending = []
    for g in range(INPROJ_GROUPS):
        pending.append(row_group(slice(g * rows, (g + 1) * rows)))
        for gen in list(pending):
            if next(gen, "done") == "done":
                pending.remove(gen)
    while pending:
        for gen in list(pending):
            if next(gen, "done") == "done":
                pending.remove(gen)


def _inproj(x2d, cos_tab, sin_tab, gains, weights, prompt_extra=None):
    n, d = x2d.shape
    tm = INPROJ_BLOCK
    prompt = prompt_extra is not None
    tab_blocks = cos_tab.shape[0] // tm
    tab_spec = pl.BlockSpec((tm, LANE), lambda i: (i % tab_blocks, 0))
    in_specs = ([_rows(tm, d), tab_spec, tab_spec] + [_resident(g.shape) for g in gains]
                + [_resident(w.shape) for w in weights])
    args = [x2d, cos_tab, sin_tab, *gains, *weights]
    common = [(MLA_HEADS * HEAD_PAD, BF16), (weights[3].shape[1], F32), (ROPE_DIM, F32)]
    scratch = []
    if prompt:
        w_uk, w_uvt, mem_k, mem_v, gw, gb = prompt_extra
        n_mem = mem_k.shape[0] // (n // (tab_blocks * tm))
        mem_spec = pl.BlockSpec((n_mem, mem_k.shape[1]), lambda i: (i // tab_blocks, 0))
        in_specs += [_resident(w_uk.shape), _resident(w_uvt.shape), mem_spec, mem_spec,
                     _resident(gw.shape), _resident(gb.shape)]
        args += [w_uk, w_uvt, mem_k, mem_v, gw, gb]
        outs = [(d, BF16), (MEM_HEADS * MEM_HEAD_DIM, BF16)] + common + [(3 * d, BF16),
                                                                        (MLA_HEADS * NOPE_DIM, BF16), None, (LANE, BF16)]
        scratch = [pltpu.VMEM((tm, d), BF16)]
    else:
        outs = [(d, BF16), (d, BF16)] + common + [(MEM_HEADS * MEM_HEAD_DIM, BF16), (3 * d, BF16), (d, F32)]
    out_specs = [w and _rows(tm, w[0]) for w in outs]
    out_shape = [w and jax.ShapeDtypeStruct((n, w[0]), w[1]) for w in outs]
    if prompt:
        hv = MLA_HEADS * V_DIM
        slot = outs.index(None)
        out_specs[slot] = pl.BlockSpec((tm // KEY_SLAB, hv, KEY_SLAB), lambda i: (i, 0, 0))
        out_shape[slot] = jax.ShapeDtypeStruct((n // KEY_SLAB, hv, KEY_SLAB), BF16)
    return pl.pallas_call(
        functools.partial(_inproj_kernel, prompt=prompt),
        grid=(n // tm,),
        in_specs=in_specs,
        out_specs=out_specs,
        out_shape=out_shape,
        scratch_shapes=scratch,
        compiler_params=_params(1),
        name="inproj",
    )(*args)


def _tril_weights(gw_ref, g, rows):
    w = gw_ref[g][:rows, :rows]
    r = lax.broadcasted_iota(jnp.int32, (rows, rows), 0)
    c = lax.broadcasted_iota(jnp.int32, (rows, rows), 1)
    return jnp.where(c <= r, w, 0.0).astype(BF16)


def _mem_attend(qm_ref, mk_ref, mv_ref, o_ref, col0):
    for h in range(MEM_HEADS):
        sl = slice(h * MEM_HEAD_DIM, (h + 1) * MEM_HEAD_DIM)
        s = _dot_nt(qm_ref[:, sl], mk_ref[:, sl])
        p = jnp.exp2(s - jnp.max(s, axis=1, keepdims=True))
        o = _dot(p.astype(BF16), mv_ref[:, sl]) / jnp.sum(p, axis=1, keepdims=True)
        o_ref[:, col0 + h * MEM_HEAD_DIM:col0 + (h + 1) * MEM_HEAD_DIM] = o.astype(BF16)


def _mix_prompt_kernel(q_ref, kn_ref, vt_ref, krp_ref, o_ref, m_sc, acc_sc):
    qi = pl.program_id(1)
    tq = q_ref.shape[0]

    half = tq // 2
    kc = lax.broadcasted_iota(jnp.int32, (half, half), 0) // CHUNK
    qc = lax.broadcasted_iota(jnp.int32, (half, half), 1) // CHUNK
    visible = kc <= qc

    ones_rows = jnp.ones((SUM_ROWS, tq), BF16)

    def key_blocks(js, first=False):
        items = [(b, h) for b in range(len(js)) for h in range(MLA_HEADS)]
        scores, probs = {}, {}

        def stage_scores(item):
            b, h = item
            rows = pl.ds(pl.multiple_of(js[b] * tq, tq), tq)
            kh = jnp.concatenate([kn_ref[rows, h * LANE:(h + 1) * LANE], krp_ref[rows, :]], axis=1)
            qh = q_ref[:, h * HEAD_PAD:(h + 1) * HEAD_PAD]
            if first:
                scores[item] = (_dot_nt(kh[:half], qh[:half]), _dot_nt(kh, qh[half:]))
            else:
                scores[item] = _dot_nt(kh, qh)

        def stage_softmax(item):
            h = item[1]
            st = scores.pop(item)
            if first:
                s0, s1 = st
                s0 = jnp.where(visible, s0, -jnp.inf)
                s1 = jnp.concatenate([s1[:half], jnp.where(visible, s1[half:], -jnp.inf)], axis=0)
                m0 = jnp.max(s0, axis=0, keepdims=True)
                m1 = jnp.max(s1, axis=0, keepdims=True)
                m_sc[h] = jnp.concatenate([m0, m1], axis=1)
                probs[item] = (jnp.exp2(s0 - m0).astype(BF16), jnp.exp2(s1 - m1).astype(BF16))
            else:
                m_old = m_sc[h]
                m = jnp.maximum(m_old, jnp.max(st, axis=0, keepdims=True))
                m_sc[h] = m
                probs[item] = (jnp.exp2(st - m).astype(BF16), jnp.exp2(m_old - m))

        def stage_values(item):
            j, h = js[item[0]], item[1]
            slabs = tq // KEY_SLAB
            vt = jnp.concatenate([vt_ref[j * slabs + i, h * V_DIM:(h + 1) * V_DIM, :] for i in range(slabs)], axis=1)
            vt = jnp.concatenate([vt, ones_rows], axis=0)
            if first:
                p0, p1 = probs.pop(item)
                acc_sc[h] = jnp.concatenate([_dot(vt[:, :half], p0), _dot(vt, p1)], axis=1)
            else:
                p, a = probs.pop(item)
                acc_sc[h] = a * acc_sc[h] + _dot(vt, p)

        for step in range(len(items) + VALUES_LAG):
            if step < len(items):
                stage_scores(items[step])
            if 0 <= step - SOFTMAX_LAG < len(items):
                stage_softmax(items[step - SOFTMAX_LAG])
            if 0 <= step - VALUES_LAG < len(items):
                stage_values(items[step - VALUES_LAG])

    key_blocks([qi], first=True)

    def past_block(j, carry):
        key_blocks([j])
        return carry

    lax.fori_loop(0, qi, past_block, 0)
    for h in range(MLA_HEADS):
        acc = acc_sc[h]
        o = acc[:V_DIM, :] * (1.0 / acc[V_DIM:V_DIM + 1, :])
        o_ref[:, h * V_DIM:(h + 1) * V_DIM] = o.T.astype(BF16)


def _mix_prompt(batch, seq, q, kn, vt, krp):
    n = q.shape[0]
    tq = Q_BLOCK
    nq = seq // tq
    hv = MLA_HEADS * V_DIM
    blk = lambda w: pl.BlockSpec((tq, w), lambda b, i: (b * nq + i, 0))
    per_batch = lambda rows, w: pl.BlockSpec((rows, w), lambda b, i: (b, 0))
    return pl.pallas_call(
        _mix_prompt_kernel,
        grid=(batch, nq),
        in_specs=[blk(q.shape[1]), per_batch(seq, kn.shape[1]),
                  pl.BlockSpec((seq // KEY_SLAB,) + vt.shape[1:], lambda b, i: (b, 0, 0)),
                  per_batch(seq, LANE)],
        out_specs=blk(hv),
        out_shape=jax.ShapeDtypeStruct((n, hv), BF16),
        scratch_shapes=[pltpu.VMEM((MLA_HEADS, 1, tq), F32), pltpu.VMEM((MLA_HEADS, V_DIM + SUM_ROWS, tq), F32)],
        compiler_params=_params(2),
        name="mix_prompt",
    )(q, kn, vt, krp)


def _mix_sample_kernel(u_ref, vn_ref, q_ref, qm_ref, ckvp_ref, krp_ref, ckvn_ref, krn_ref, wuk_ref, wuv_ref,
                       mk_ref, mv_ref, gw_ref, gb_ref, o_ref):
    ts, d = u_ref.shape

    for g in range(GM_GROUPS):
        cs = slice(g * LANE, (g + 1) * LANE)
        mixed = _dot(_tril_weights(gw_ref, g, ts), vn_ref[:, cs]) + gb_ref[g][:ts, :]
        o_ref[:, cs] = (u_ref[:, cs].astype(F32) * mixed).astype(BF16)

    def keys(c_ref, r_ref):
        r = r_ref[...]
        c = c_ref[...].astype(BF16)
        return c, jnp.concatenate([c, jnp.concatenate([r, jnp.zeros_like(r)], axis=1).astype(BF16)], axis=1)

    cp, kp = keys(ckvp_ref, krp_ref)
    cn, kn = keys(ckvn_ref, krn_ref)
    qs = []
    for h in range(MLA_HEADS):
        q_abs = _dot_nt(q_ref[:, h * HEAD_PAD:h * HEAD_PAD + NOPE_DIM], wuk_ref[:, h * NOPE_DIM:(h + 1) * NOPE_DIM])
        qs.append(jnp.concatenate([q_abs.astype(BF16), q_ref[:, h * HEAD_PAD + NOPE_DIM:(h + 1) * HEAD_PAD]], axis=1))
    qa = jnp.concatenate(qs, axis=0)
    s_past = _dot_nt(qa, kp)
    s_new = _dot_nt(qa, kn)
    m = jnp.maximum(jnp.max(s_past, axis=1, keepdims=True), jnp.max(s_new, axis=1, keepdims=True))
    p_past = jnp.exp2(s_past - m)
    p_new = jnp.exp2(s_new - m)
    l = jnp.sum(p_past, axis=1, keepdims=True) + jnp.sum(p_new, axis=1, keepdims=True)
    lat = ((_dot(p_past.astype(BF16), cp) + _dot(p_new.astype(BF16), cn)) / l).astype(BF16)
    for h in range(MLA_HEADS):
        o_ref[:, d + h * V_DIM:d + (h + 1) * V_DIM] = _dot(lat[h * ts:(h + 1) * ts, :],
                                                          wuv_ref[:, h * V_DIM:(h + 1) * V_DIM]).astype(BF16)

    _mem_attend(qm_ref, mk_ref, mv_ref, o_ref, d + MLA_HEADS * V_DIM)


def _mix_sample(batch, ts, past, u, vn, q, qm, ckv_past, kr_past, ckv_new, kr_new, w_uk, w_uv, mkb, mvb, gw, gb):
    n, d = u.shape
    n_mem = mkb.shape[0] // batch
    per_batch = lambda rows, w: pl.BlockSpec((rows, w), lambda b: (b, 0))
    return pl.pallas_call(
        _mix_sample_kernel,
        grid=(batch,),
        in_specs=[per_batch(ts, d), per_batch(ts, d), per_batch(ts, q.shape[1]), per_batch(ts, qm.shape[1]),
                  per_batch(past, ckv_past.shape[1]), per_batch(past, ROPE_DIM),
                  per_batch(ts, ckv_new.shape[1]), per_batch(ts, ROPE_DIM),
                  _resident(w_uk.shape), _resident(w_uv.shape),
                  per_batch(n_mem, mkb.shape[1]), per_batch(n_mem, mvb.shape[1]),
                  _resident(gw.shape), _resident(gb.shape)],
        out_specs=per_batch(ts, 3 * d),
        out_shape=jax.ShapeDtypeStruct((n, 3 * d), BF16),
        compiler_params=_params(1),
        name="mix_sample",
    )(u, vn, q, qm, ckv_past, kr_past, ckv_new, kr_new, w_uk, w_uv, mkb, mvb, gw, gb)


def _merge_ffn_kernel(x_ref, ogm_ref, omla_ref, omem_ref, gate_ref, gffn_ref, gfin_ref, wgm_ref, wmla_ref, wmem_ref,
                      wout_ref, wfg_ref, wfu_ref, wfd_ref, y_ref):
    tm, d = x_ref.shape

    def merge(r):
        merged = (gate_ref[r, 0:d].astype(F32) * _dot(ogm_ref[r, :], wgm_ref[...])
                  + gate_ref[r, d:2 * d].astype(F32) * _dot(omla_ref[r, :], wmla_ref[...])
                  + gate_ref[r, 2 * d:3 * d].astype(F32) * _dot(omem_ref[r, :], wmem_ref[...]))
        return x_ref[r, :] + _dot(merged.astype(BF16), wout_ref[...])

    def ffn_hidden(h):
        hn = _rms(h, gffn_ref[...]).astype(BF16)
        return (jax.nn.silu(_dot(hn, wfg_ref[...])) * _dot(hn, wfu_ref[...])).astype(BF16)

    def finish(r, h, a):
        y_ref[r, :] = _rms(h + _dot(a, wfd_ref[...]), gfin_ref[...])

    groups = [slice(i * tm // MERGE_GROUPS, (i + 1) * tm // MERGE_GROUPS) for i in range(MERGE_GROUPS)]
    hs = [merge(r) for r in groups]
    acts = [ffn_hidden(h) for h in hs]
    for r, h, a in zip(groups, hs, acts):
        finish(r, h, a)


def _merge_ffn(x2d, branches, gates, g_ffn, g_fin, weights):
    n, d = x2d.shape
    tm = MERGE_BLOCK
    if isinstance(branches, (tuple, list)):
        branch_specs = [_rows(tm, d)] * 3
    else:
        branch_specs = [pl.BlockSpec((tm, d), lambda i, c=c: (i, c)) for c in range(3)]
        branches = (branches,) * 3
    return pl.pallas_call(
        _merge_ffn_kernel,
        grid=(n // tm,),
        in_specs=([_rows(tm, d)] + branch_specs + [_rows(tm, 3 * d), _resident(g_ffn.shape), _resident(g_fin.shape)]
                  + [_resident(w.shape) for w in weights]),
        out_specs=_rows(tm, d),
        out_shape=jax.ShapeDtypeStruct((n, d), F32),
        compiler_params=_params(1),
        name="merge_ffn",
    )(x2d, *branches, gates, g_ffn, g_fin, *weights)


def _rope_tables(pos):
    inv = ROPE_THETA ** (-jnp.arange(0, ROPE_DIM, 2, dtype=F32) / ROPE_DIM)
    ang = pos.astype(F32)[:, None] * inv[None, :]
    cos, sin = jnp.cos(ang), jnp.sin(ang)
    return jnp.concatenate([cos, cos, cos, cos], axis=1), jnp.concatenate([-sin, sin, -sin, sin], axis=1)


def _swap_halves(w):
    half = w.shape[-1] // 2
    return jnp.concatenate([w[..., half:], w[..., :half]], axis=-1)


def kernel(x_prompt, x_sample, cache_mla_ckv, cache_mla_krope, cache_mem_k, cache_mem_v, mem_prompt, norm_mix_g, w_in, gm_norm_g, gm_ws, gm_bs, mla_q_norm_g, mla_w_uq, mla_kv_norm_g, mla_w_uk, mla_w_uv, mem_norm_g, mem_w_kv, w_br_gm, w_br_mla, w_br_mem, w_out, norm_ffn_g, ffn_w_gate, ffn_w_up, ffn_w_down, final_norm_g):
    depth = w_in.shape[0]
    assert depth == 1, "single-layer step"
    batch, seq, d = x_prompt.shape
    dec_batch, dec_seq, _ = x_sample.shape
    past = cache_mla_ckv.shape[2]
    n_mem = mem_prompt.shape[1]
    q_lora = mla_w_uq.shape[1]
    kv_lora = mla_w_uk.shape[1]
    d_mem = MEM_HEADS * MEM_HEAD_DIM
    assert seq % Q_BLOCK == 0 and seq % INPROJ_BLOCK == 0 and (batch * seq) % MERGE_BLOCK == 0
    assert (dec_batch * dec_seq) % INPROJ_BLOCK == 0 and (dec_batch * dec_seq) % MERGE_BLOCK == 0
    assert INPROJ_BLOCK % dec_seq == 0 and past % GM_CHUNK == 0 and (batch * n_mem) % TOKEN_BLOCK == 0
    assert past % CHUNK == 0 and dec_seq <= CHUNK, "new tokens must sit in one attention chunk"

    w = w_in[0]
    o = 0
    parts = []
    for width in (d, d, q_lora, kv_lora, ROPE_DIM, d_mem, 3 * d):
        parts.append(w[:, o:o + width])
        o += width
    w_u, w_v, w_cq, w_ckv, w_kr, w_qm, w_gate = parts
    w_c = jnp.concatenate([w_cq, w_kr, _swap_halves(w_kr)], axis=1)
    uq = mla_w_uq[0]
    uq_rope = uq[:, :, NOPE_DIM:]
    w_q3 = jnp.concatenate([uq[:, :, :NOPE_DIM].reshape(q_lora, -1),
                            uq_rope.reshape(q_lora, -1),
                            _swap_halves(uq_rope).reshape(q_lora, -1)], axis=1)
    w_uk = mla_w_uk[0].reshape(kv_lora, -1).astype(BF16)
    w_uv = mla_w_uv[0].reshape(kv_lora, -1).astype(BF16)
    in_weights = [x.astype(BF16) for x in (w_u, w_v, w_c, w_ckv, w_qm, w_gate, w_q3)]
    in_gains = [norm_mix_g[0][None], gm_norm_g[0][None], mla_q_norm_g[0][None], mla_kv_norm_g[0][None]]
    out_weights = [x[0].astype(BF16) for x in (w_br_gm, w_br_mla, w_br_mem, w_out, ffn_w_gate, ffn_w_up, ffn_w_down)]
    gw = gm_ws[0]
    gb = jnp.broadcast_to(gm_bs[0][:, :, None], gm_bs.shape[1:] + (LANE,))

    mk, mv, mkb, mvb = _memkv(mem_prompt.reshape(batch * n_mem, d), mem_norm_g[0][None], mem_w_kv[0].astype(BF16))
    xp = x_prompt.reshape(batch * seq, d)
    cos_p, sin_p = _rope_tables(jnp.arange(seq, dtype=jnp.int32))
    o_gm, o_mem, q, ckv_p, kr_p, gates, kn, vt, krp = _inproj(xp, cos_p, sin_p, in_gains, in_weights,
                                                              (w_uk, w_uv.T, mkb, mvb, gw, gb))
    o_mla = _mix_prompt(batch, seq, q, kn, vt, krp)
    y_prompt = _merge_ffn(xp, (o_gm, o_mla, o_mem), gates, norm_ffn_g[0][None], final_norm_g[None], out_weights)

    xs = x_sample.reshape(dec_batch * dec_seq, d)
    cos_s, sin_s = _rope_tables(past + jnp.arange(dec_seq, dtype=jnp.int32))
    reps = max(INPROJ_BLOCK // dec_seq, 1)
    cos_s, sin_s = jnp.tile(cos_s, (reps, 1)), jnp.tile(sin_s, (reps, 1))
    u, vn, q, ckv_s, kr_s, qm, gates, gv_s = _inproj(xs, cos_s, sin_s, in_gains, in_weights)
    cmk = cache_mem_k[0].reshape(dec_batch * n_mem, d_mem).astype(BF16)
    cmv = cache_mem_v[0].reshape(dec_batch * n_mem, d_mem).astype(BF16)
    o_cat = _mix_sample(dec_batch, dec_seq, past, u, vn, q, qm,
                        cache_mla_ckv[0].reshape(dec_batch * past, kv_lora),
                        cache_mla_krope[0].reshape(dec_batch * past, ROPE_DIM), ckv_s, kr_s, w_uk, w_uv,
                        cmk, cmv, gw, gb)
    y_sample = _merge_ffn(xs, o_cat, gates, norm_ffn_g[0][None], final_norm_g[None], out_weights)

    return (y_prompt.reshape(batch, seq, d), y_sample.reshape(dec_batch, dec_seq, d),
            ckv_p.reshape(1, batch, seq, kv_lora), kr_p.reshape(1, batch, seq, ROPE_DIM),
            mk.reshape(1, batch, n_mem, MEM_HEADS, MEM_HEAD_DIM), mv.reshape(1, batch, n_mem, MEM_HEADS, MEM_HEAD_DIM),
            ckv_s.reshape(1, dec_batch, dec_seq, kv_lora), kr_s.reshape(1, dec_batch, dec_seq, ROPE_DIM),
            gv_s.reshape(1, dec_batch, dec_seq, d))
```

```python
import functools
import math

import jax
import jax.numpy as jnp
from jax import lax
from jax.experimental import pallas as pl
from jax.experimental.pallas import tpu as pltpu

F32 = jnp.float32
BF16 = jnp.bfloat16

NORM_EPS = 1e-6
CHUNK = 64
GM_CHUNK = 128
GM_GROUPS = 8
MLA_HEADS = 8
NOPE_DIM = 128
ROPE_DIM = 64
V_DIM = 128
ROPE_THETA = 10000.0
MEM_HEADS = 4
MEM_HEAD_DIM = 256
LOG2E = math.log2(math.e)
MLA_QSCALE = (NOPE_DIM + ROPE_DIM) ** -0.5 * LOG2E
MEM_QSCALE = MEM_HEAD_DIM ** -0.5 * LOG2E

LANE = 128
HEAD_PAD = 2 * LANE
SUM_ROWS = 16
SOFTMAX_LAG = 2
VALUES_LAG = 4
VMEM_LIMIT_BYTES = 56 * 1024 * 1024

TOKEN_BLOCK = 512
INPROJ_BLOCK = 512
INPROJ_GROUPS = 2
MERGE_BLOCK = 512
MERGE_GROUPS = 2
Q_BLOCK = 512
KEY_SLAB = 256


def _params(n_axes):
    return pltpu.CompilerParams(dimension_semantics=("arbitrary",) * n_axes,
                                vmem_limit_bytes=VMEM_LIMIT_BYTES)


def _resident(shape):
    return pl.BlockSpec(shape, lambda *_: (0,) * len(shape), pipeline_mode=pl.Buffered(1))


def _rows(tm, width):
    return pl.BlockSpec((tm, width), lambda i: (i, 0))


def _dot(a, b):
    return jnp.dot(a, b, preferred_element_type=F32)


def _dot_nt(a, b):
    return lax.dot_general(a, b, (((1,), (1,)), ((), ())), preferred_element_type=F32)


def _rms(x, g):
    return x * lax.rsqrt(jnp.mean(x * x, axis=-1, keepdims=True) + NORM_EPS) * g


def _memkv_kernel(mem_ref, g_ref, w_ref, k_ref, v_ref, kb_ref, vb_ref):
    d = kb_ref.shape[1]
    kv = _dot(_rms(mem_ref[...], g_ref[...]).astype(BF16), w_ref[...])
    k, v = kv[:, :d], kv[:, d:]
    for h in range(MEM_HEADS):
        k_ref[:, h, :] = k[:, h * MEM_HEAD_DIM:(h + 1) * MEM_HEAD_DIM]
        v_ref[:, h, :] = v[:, h * MEM_HEAD_DIM:(h + 1) * MEM_HEAD_DIM]
    kb_ref[...] = k.astype(BF16)
    vb_ref[...] = v.astype(BF16)


def _memkv(mem2d, g, w_kv):
    n, d = mem2d.shape
    dm = w_kv.shape[1] // 2
    tm = TOKEN_BLOCK
    return pl.pallas_call(
        _memkv_kernel,
        grid=(n // tm,),
        in_specs=[_rows(tm, d), _resident((1, d)), _resident(w_kv.shape)],
        out_specs=[pl.BlockSpec((tm, MEM_HEADS, MEM_HEAD_DIM), lambda i: (i, 0, 0))] * 2 + [_rows(tm, dm)] * 2,
        out_shape=([jax.ShapeDtypeStruct((n, MEM_HEADS, MEM_HEAD_DIM), F32)] * 2
                   + [jax.ShapeDtypeStruct((n, dm), BF16)] * 2),
        compiler_params=_params(1),
        name="memkv",
    )(mem2d, g, w_kv)


def _inproj_kernel(x_ref, cos_ref, sin_ref, gmix_ref, ggm_ref, gq_ref, gkv_ref,
                   wu_ref, wv_ref, wc_ref, wckv_ref, wqm_ref, wg_ref, wq3_ref, *rest, prompt):
    if prompt:
        (wuk_ref, wuvt_ref, mk_ref, mv_ref, gw_ref, gb_ref,
         ogm_ref, omem_ref, q_ref, ckv_ref, kr_ref, gate_ref, kn_ref, vt_ref, krp_ref, u_ref) = rest
    else:
        u_ref, vn_ref, q_ref, ckv_ref, kr_ref, qm_ref, gate_ref, vn32_ref = rest
    tm, d = x_ref.shape
    q_lora = wq3_ref.shape[0]
    nn, nr = MLA_HEADS * NOPE_DIM, MLA_HEADS * ROPE_DIM
    low = lax.broadcasted_iota(jnp.int32, (1, LANE), 1) < ROPE_DIM

    def row_group(r):
        xn = _rms(x_ref[r, :], gmix_ref[...]).astype(BF16)
        cos = cos_ref[r, :]
        sin = sin_ref[r, :]

        def gate_chunk(c):
            cols = slice(c * d, (c + 1) * d)
            gate_ref[r, cols] = jax.nn.sigmoid(_dot(xn, wg_ref[:, cols])).astype(BF16)

        zc = _dot(xn, wc_ref[...])
        zkv = _dot(xn, wckv_ref[...])
        zu = _dot(xn, wu_ref[...])
        yield

        cqn = _rms(zc[:, :q_lora], gq_ref[...]).astype(BF16)
        ckv = _rms(zkv, gkv_ref[...])
        ckv_ref[r, :] = ckv
        krk = zc[:, q_lora:]
        krr = krk * jnp.where(low, cos, 0.0) + pltpu.roll(krk, ROPE_DIM, 1) * jnp.where(low, sin, 0.0)
        kr_ref[r, :] = krr[:, :ROPE_DIM]
        q3 = _dot(cqn, wq3_ref[...])
        gate_chunk(0)
        yield

        u_ref[r, :] = jax.nn.gelu(zu).astype(BF16)
        zv = _dot(xn, wv_ref[...])
        if prompt:
            ckvb = ckv.astype(BF16)
            kn_ref[r, :] = _dot(ckvb, wuk_ref[...]).astype(BF16)
            step = min(KEY_SLAB, ckvb.shape[0])
            for lo in range(0, ckvb.shape[0], step):
                k0 = r.start + lo
                vt_ref[k0 // KEY_SLAB, :, k0 % KEY_SLAB:k0 % KEY_SLAB + step] = _dot_nt(
                    wuvt_ref[...], ckvb[lo:lo + step, :]).astype(BF16)
            krp_ref[r, :] = krr.astype(BF16)
        yield

        zeros = jnp.zeros((q3.shape[0], LANE - ROPE_DIM), F32)
        for pair in range(MLA_HEADS // 2):
            lo = pair * LANE
            rot = (q3[:, nn + lo:nn + lo + LANE] * cos + q3[:, nn + nr + lo:nn + nr + lo + LANE] * sin) * MLA_QSCALE
            for h, piece in ((2 * pair, rot[:, :ROPE_DIM]), (2 * pair + 1, rot[:, ROPE_DIM:])):
                q_ref[r, h * HEAD_PAD:h * HEAD_PAD + LANE] = (q3[:, h * LANE:(h + 1) * LANE]
                                                              * MLA_QSCALE).astype(BF16)
                q_ref[r, h * HEAD_PAD + LANE:(h + 1) * HEAD_PAD] = jnp.concatenate([piece, zeros],
                                                                                   axis=1).astype(BF16)
        qm = (_dot(xn, wqm_ref[...]) * MEM_QSCALE).astype(BF16)
        if prompt:
            for h in range(MEM_HEADS):
                sl = slice(h * MEM_HEAD_DIM, (h + 1) * MEM_HEAD_DIM)
                sc = _dot_nt(qm[:, sl], mk_ref[:, sl])
                p = jnp.exp2(sc - jnp.max(sc, axis=1, keepdims=True))
                o = _dot(p.astype(BF16), mv_ref[:, sl]) / jnp.sum(p, axis=1, keepdims=True)
                omem_ref[r, sl] = o.astype(BF16)
        else:
            qm_ref[r, :] = qm
        gate_chunk(1)
        yield

        vn = _rms(jax.nn.gelu(zv), ggm_ref[...])
        if prompt:
            vnb = vn.astype(BF16)
            for g in range(GM_GROUPS):
                wg = _tril_weights(gw_ref, g, GM_CHUNK)
                bias = gb_ref[g]
                cs = slice(g * LANE, (g + 1) * LANE)
                for c0 in range(0, vnb.shape[0], 2 * GM_CHUNK):
                    mixed = _dot(wg, jnp.concatenate([vnb[c0:c0 + GM_CHUNK, cs],
                                                      vnb[c0 + GM_CHUNK:c0 + 2 * GM_CHUNK, cs]], axis=1))
                    for k in range(2):
                        rows = slice(r.start + c0 + k * GM_CHUNK, r.start + c0 + (k + 1) * GM_CHUNK)
                        ogm_ref[rows, cs] = (u_ref[rows, cs].astype(F32)
                                             * (mixed[:, k * LANE:(k + 1) * LANE] + bias)).astype(BF16)
        else:
            vn_ref[r, :] = vn.astype(BF16)
            vn32_ref[r, :] = vn
        yield

        gate_chunk(2)

    rows = tm // INPROJ_GROUPS
    pending = []
    for g in range(INPROJ_GROUPS):
        pending.append(row_group(slice(g * rows, (g + 1) * rows)))
        for gen in list(pending):
            if next(gen, "done") == "done":
                pending.remove(gen)
    while pending:
        for gen in list(pending):
            if next(gen, "done") == "done":
                pending.remove(gen)


def _inproj(x2d, cos_tab, sin_tab, gains, weights, prompt_extra=None):
    n, d = x2d.shape
    tm = INPROJ_BLOCK
    prompt = prompt_extra is not None
    tab_blocks = cos_tab.shape[0] // tm
    tab_spec = pl.BlockSpec((tm, LANE), lambda i: (i % tab_blocks, 0))
    in_specs = ([_rows(tm, d), tab_spec, tab_spec] + [_resident(g.shape) for g in gains]
                + [_resident(w.shape) for w in weights])
    args = [x2d, cos_tab, sin_tab, *gains, *weights]
    common = [(MLA_HEADS * HEAD_PAD, BF16), (weights[3].shape[1], F32), (ROPE_DIM, F32)]
    scratch = []
    if prompt:
        w_uk, w_uvt, mem_k, mem_v, gw, gb = prompt_extra
        n_mem = mem_k.shape[0] // (n // (tab_blocks * tm))
        mem_spec = pl.BlockSpec((n_mem, mem_k.shape[1]), lambda i: (i // tab_blocks, 0))
        in_specs += [_resident(w_uk.shape), _resident(w_uvt.shape), mem_spec, mem_spec,
                     _resident(gw.shape), _resident(gb.shape)]
        args += [w_uk, w_uvt, mem_k, mem_v, gw, gb]
        outs = [(d, BF16), (MEM_HEADS * MEM_HEAD_DIM, BF16)] + common + [(3 * d, BF16),
                                                                        (MLA_HEADS * NOPE_DIM, BF16), None, (LANE, BF16)]
        scratch = [pltpu.VMEM((tm, d), BF16)]
    else:
        outs = [(d, BF16), (d, BF16)] + common + [(MEM_HEADS * MEM_HEAD_DIM, BF16), (3 * d, BF16), (d, F32)]
    out_specs = [w and _rows(tm, w[0]) for w in outs]
    out_shape = [w and jax.ShapeDtypeStruct((n, w[0]), w[1]) for w in outs]
    if prompt:
        hv = MLA_HEADS * V_DIM
        slot = outs.index(None)
        out_specs[slot] = pl.BlockSpec((tm // KEY_SLAB, hv, KEY_SLAB), lambda i: (i, 0, 0))
        out_shape[slot] = jax.ShapeDtypeStruct((n // KEY_SLAB, hv, KEY_SLAB), BF16)
    return pl.pallas_call(
        functools.partial(_inproj_kernel, prompt=prompt),
        grid=(n // tm,),
        in_specs=in_specs,
        out_specs=out_specs,
        out_shape=out_shape,
        scratch_shapes=scratch,
        compiler_params=_params(1),
        name="inproj",
    )(*args)


def _tril_weights(gw_ref, g, rows):
    w = gw_ref[g][:rows, :rows]
    r = lax.broadcasted_iota(jnp.int32, (rows, rows), 0)
    c = lax.broadcasted_iota(jnp.int32, (rows, rows), 1)
    return jnp.where(c <= r, w, 0.0).astype(BF16)


def _mem_attend(qm_ref, mk_ref, mv_ref, o_ref, col0):
    for h in range(MEM_HEADS):
        sl = slice(h * MEM_HEAD_DIM, (h + 1) * MEM_HEAD_DIM)
        s = _dot_nt(qm_ref[:, sl], mk_ref[:, sl])
        p = jnp.exp2(s - jnp.max(s, axis=1, keepdims=True))
        o = _dot(p.astype(BF16), mv_ref[:, sl]) / jnp.sum(p, axis=1, keepdims=True)
        o_ref[:, col0 + h * MEM_HEAD_DIM:col0 + (h + 1) * MEM_HEAD_DIM] = o.astype(BF16)


def _mix_prompt_kernel(q_ref, kn_ref, vt_ref, krp_ref, o_ref, m_sc, acc_sc):
    qi = pl.program_id(1)
    tq = q_ref.shape[0]

    half = tq // 2
    kc = lax.broadcasted_iota(jnp.int32, (half, half), 0) // CHUNK
    qc = lax.broadcasted_iota(jnp.int32, (half, half), 1) // CHUNK
    visible = kc <= qc

    ones_rows = jnp.ones((SUM_ROWS, tq), BF16)

    def key_blocks(js, first=False):
        items = [(b, h) for b in range(len(js)) for h in range(MLA_HEADS)]
        scores, probs = {}, {}

        def stage_scores(item):
            b, h = item
            rows = pl.ds(pl.multiple_of(js[b] * tq, tq), tq)
            kh = jnp.concatenate([kn_ref[rows, h * LANE:(h + 1) * LANE], krp_ref[rows, :]], axis=1)
            qh = q_ref[:, h * HEAD_PAD:(h + 1) * HEAD_PAD]
            if first:
                scores[item] = (_dot_nt(kh[:half], qh[:half]), _dot_nt(kh, qh[half:]))
            else:
                scores[item] = _dot_nt(kh, qh)

        def stage_softmax(item):
            h = item[1]
            st = scores.pop(item)
            if first:
                s0, s1 = st
                s0 = jnp.where(visible, s0, -jnp.inf)
                s1 = jnp.concatenate([s1[:half], jnp.where(visible, s1[half:], -jnp.inf)], axis=0)
                m0 = jnp.max(s0, axis=0, keepdims=True)
                m1 = jnp.max(s1, axis=0, keepdims=True)
                m_sc[h] = jnp.concatenate([m0, m1], axis=1)
                probs[item] = (jnp.exp2(s0 - m0).astype(BF16), jnp.exp2(s1 - m1).astype(BF16))
            else:
                m_old = m_sc[h]
                m = jnp.maximum(m_old, jnp.max(st, axis=0, keepdims=True))
                m_sc[h] = m
                probs[item] = (jnp.exp2(st - m).astype(BF16), jnp.exp2(m_old - m))

        def stage_values(item):
            j, h = js[item[0]], item[1]
            slabs = tq // KEY_SLAB
            vt = jnp.concatenate([vt_ref[j * slabs + i, h * V_DIM:(h + 1) * V_DIM, :] for i in range(slabs)], axis=1)
            vt = jnp.concatenate([vt, ones_rows], axis=0)
            if first:
                p0, p1 = probs.pop(item)
                acc_sc[h] = jnp.concatenate([_dot(vt[:, :half], p0), _dot(vt, p1)], axis=1)
            else:
                p, a = probs.pop(item)
                acc_sc[h] = a * acc_sc[h] + _dot(vt, p)

        for step in range(len(items) + VALUES_LAG):
            if step < len(items):
                stage_scores(items[step])
            if 0 <= step - SOFTMAX_LAG < len(items):
                stage_softmax(items[step - SOFTMAX_LAG])
            if 0 <= step - VALUES_LAG < len(items):
                stage_values(items[step - VALUES_LAG])

    key_blocks([qi], first=True)

    def past_block(j, carry):
        key_blocks([j])
        return carry

    lax.fori_loop(0, qi, past_block, 0)
    for h in range(MLA_HEADS):
        acc = acc_sc[h]
        o = acc[:V_DIM, :] * (1.0 / acc[V_DIM:V_DIM + 1, :])
        o_ref[:, h * V_DIM:(h + 1) * V_DIM] = o.T.astype(BF16)


def _mix_prompt(batch, seq, q, kn, vt, krp):
    n = q.shape[0]
    tq = Q_BLOCK
    nq = seq // tq
    hv = MLA_HEADS * V_DIM
    blk = lambda w: pl.BlockSpec((tq, w), lambda b, i: (b * nq + i, 0))
    per_batch = lambda rows, w: pl.BlockSpec((rows, w), lambda b, i: (b, 0))
    return pl.pallas_call(
        _mix_prompt_kernel,
        grid=(batch, nq),
        in_specs=[blk(q.shape[1]), per_batch(seq, kn.shape[1]),
                  pl.BlockSpec((seq // KEY_SLAB,) + vt.shape[1:], lambda b, i: (b, 0, 0)),
                  per_batch(seq, LANE)],
        out_specs=blk(hv),
        out_shape=jax.ShapeDtypeStruct((n, hv), BF16),
        scratch_shapes=[pltpu.VMEM((MLA_HEADS, 1, tq), F32), pltpu.VMEM((MLA_HEADS, V_DIM + SUM_ROWS, tq), F32)],
        compiler_params=_params(2),
        name="mix_prompt",
    )(q, kn, vt, krp)


def _mix_sample_kernel(u_ref, vn_ref, q_ref, qm_ref, ckvp_ref, krp_ref, ckvn_ref, krn_ref, wuk_ref, wuv_ref,
                       mk_ref, mv_ref, gw_ref, gb_ref, o_ref):
    ts, d = u_ref.shape

    for g in range(GM_GROUPS):
        cs = slice(g * LANE, (g + 1) * LANE)
        mixed = _dot(_tril_weights(gw_ref, g, ts), vn_ref[:, cs]) + gb_ref[g][:ts, :]
        o_ref[:, cs] = (u_ref[:, cs].astype(F32) * mixed).astype(BF16)

    def keys(c_ref, r_ref):
        r = r_ref[...]
        c = c_ref[...].astype(BF16)
        return c, jnp.concatenate([c, jnp.concatenate([r, jnp.zeros_like(r)], axis=1).astype(BF16)], axis=1)

    cp, kp = keys(ckvp_ref, krp_ref)
    cn, kn = keys(ckvn_ref, krn_ref)
    qs = []
    for h in range(MLA_HEADS):
        q_abs = _dot_nt(q_ref[:, h * HEAD_PAD:h * HEAD_PAD + NOPE_DIM], wuk_ref[:, h * NOPE_DIM:(h + 1) * NOPE_DIM])
        qs.append(jnp.concatenate([q_abs.astype(BF16), q_ref[:, h * HEAD_PAD + NOPE_DIM:(h + 1) * HEAD_PAD]], axis=1))
    qa = jnp.concatenate(qs, axis=0)
    s_past = _dot_nt(qa, kp)
    s_new = _dot_nt(qa, kn)
    m = jnp.maximum(jnp.max(s_past, axis=1, keepdims=True), jnp.max(s_new, axis=1, keepdims=True))
    p_past = jnp.exp2(s_past - m)
    p_new = jnp.exp2(s_new - m)
    l = jnp.sum(p_past, axis=1, keepdims=True) + jnp.sum(p_new, axis=1, keepdims=True)
    lat = ((_dot(p_past.astype(BF16), cp) + _dot(p_new.astype(BF16), cn)) / l).astype(BF16)
    for h in range(MLA_HEADS):
        o_ref[:, d + h * V_DIM:d + (h + 1) * V_DIM] = _dot(lat[h * ts:(h + 1) * ts, :],
                                                          wuv_ref[:, h * V_DIM:(h + 1) * V_DIM]).astype(BF16)

    _mem_attend(qm_ref, mk_ref, mv_ref, o_ref, d + MLA_HEADS * V_DIM)


def _mix_sample(batch, ts, past, u, vn, q, qm, ckv_past, kr_past, ckv_new, kr_new, w_uk, w_uv, mkb, mvb, gw, gb):
    n, d = u.shape
    n_mem = mkb.shape[0] // batch
    per_batch = lambda rows, w: pl.BlockSpec((rows, w), lambda b: (b, 0))
    return pl.pallas_call(
        _mix_sample_kernel,
        grid=(batch,),
        in_specs=[per_batch(ts, d), per_batch(ts, d), per_batch(ts, q.shape[1]), per_batch(ts, qm.shape[1]),
                  per_batch(past, ckv_past.shape[1]), per_batch(past, ROPE_DIM),
                  per_batch(ts, ckv_new.shape[1]), per_batch(ts, ROPE_DIM),
                  _resident(w_uk.shape), _resident(w_uv.shape),
                  per_batch(n_mem, mkb.shape[1]), per_batch(n_mem, mvb.shape[1]),
                  _resident(gw.shape), _resident(gb.shape)],
        out_specs=per_batch(ts, 3 * d),
        out_shape=jax.ShapeDtypeStruct((n, 3 * d), BF16),
        compiler_params=_params(1),
        name="mix_sample",
    )(u, vn, q, qm, ckv_past, kr_past, ckv_new, kr_new, w_uk, w_uv, mkb, mvb, gw, gb)


def _merge_ffn_kernel(x_ref, ogm_ref, omla_ref, omem_ref, gate_ref, gffn_ref, gfin_ref, wgm_ref, wmla_ref, wmem_ref,
                      wout_ref, wfg_ref, wfu_ref, wfd_ref, y_ref):
    tm, d = x_ref.shape

    def merge(r):
        merged = (gate_ref[r, 0:d].astype(F32) * _dot(ogm_ref[r, :], wgm_ref[...])
                  + gate_ref[r, d:2 * d].astype(F32) * _dot(omla_ref[r, :], wmla_ref[...])
                  + gate_ref[r, 2 * d:3 * d].astype(F32) * _dot(omem_ref[r, :], wmem_ref[...]))
        return x_ref[r, :] + _dot(merged.astype(BF16), wout_ref[...])

    def ffn_hidden(h):
        hn = _rms(h, gffn_ref[...]).astype(BF16)
        return (jax.nn.silu(_dot(hn, wfg_ref[...])) * _dot(hn, wfu_ref[...])).astype(BF16)

    def finish(r, h, a):
        y_ref[r, :] = _rms(h + _dot(a, wfd_ref[...]), gfin_ref[...])

    groups = [slice(i * tm // MERGE_GROUPS, (i + 1) * tm // MERGE_GROUPS) for i in range(MERGE_GROUPS)]
    hs = [merge(r) for r in groups]
    acts = [ffn_hidden(h) for h in hs]
    for r, h, a in zip(groups, hs, acts):
        finish(r, h, a)


def _merge_ffn(x2d, branches, gates, g_ffn, g_fin, weights):
    n, d = x2d.shape
    tm = MERGE_BLOCK
    if isinstance(branches, (tuple, list)):
        branch_specs = [_rows(tm, d)] * 3
    else:
        branch_specs = [pl.BlockSpec((tm, d), lambda i, c=c: (i, c)) for c in range(3)]
        branches = (branches,) * 3
    return pl.pallas_call(
        _merge_ffn_kernel,
        grid=(n // tm,),
        in_specs=([_rows(tm, d)] + branch_specs + [_rows(tm, 3 * d), _resident(g_ffn.shape), _resident(g_fin.shape)]
                  + [_resident(w.shape) for w in weights]),
        out_specs=_rows(tm, d),
        out_shape=jax.ShapeDtypeStruct((n, d), F32),
        compiler_params=_params(1),
        name="merge_ffn",
    )(x2d, *branches, gates, g_ffn, g_fin, *weights)


def _rope_tables(pos):
    inv = ROPE_THETA ** (-jnp.arange(0, ROPE_DIM, 2, dtype=F32) / ROPE_DIM)
    ang = pos.astype(F32)[:, None] * inv[None, :]
    cos, sin = jnp.cos(ang), jnp.sin(ang)
    return jnp.concatenate([cos, cos, cos, cos], axis=1), jnp.concatenate([-sin, sin, -sin, sin], axis=1)


def _swap_halves(w):
    half = w.shape[-1] // 2
    return jnp.concatenate([w[..., half:], w[..., :half]], axis=-1)


def kernel(x_prompt, x_sample, cache_mla_ckv, cache_mla_krope, cache_mem_k, cache_mem_v, mem_prompt, norm_mix_g, w_in, gm_norm_g, gm_ws, gm_bs, mla_q_norm_g, mla_w_uq, mla_kv_norm_g, mla_w_uk, mla_w_uv, mem_norm_g, mem_w_kv, w_br_gm, w_br_mla, w_br_mem, w_out, norm_ffn_g, ffn_w_gate, ffn_w_up, ffn_w_down, final_norm_g):
    depth = w_in.shape[0]
    assert depth == 1, "single-layer step"
    batch, seq, d = x_prompt.shape
    dec_batch, dec_seq, _ = x_sample.shape
    past = cache_mla_ckv.shape[2]
    n_mem = mem_prompt.shape[1]
    q_lora = mla_w_uq.shape[1]
    kv_lora = mla_w_uk.shape[1]
    d_mem = MEM_HEADS * MEM_HEAD_DIM
    assert seq % Q_BLOCK == 0 and seq % INPROJ_BLOCK == 0 and (batch * seq) % MERGE_BLOCK == 0
    assert (dec_batch * dec_seq) % INPROJ_BLOCK == 0 and (dec_batch * dec_seq) % MERGE_BLOCK == 0
    assert INPROJ_BLOCK % dec_seq == 0 and past % GM_CHUNK == 0 and (batch * n_mem) % TOKEN_BLOCK == 0
    assert past % CHUNK == 0 and dec_seq <= CHUNK, "new tokens must sit in one attention chunk"

    w = w_in[0]
    o = 0
    parts = []
    for width in (d, d, q_lora, kv_lora, ROPE_DIM, d_mem, 3 * d):
        parts.append(w[:, o:o + width])
        o += width
    w_u, w_v, w_cq, w_ckv, w_kr, w_qm, w_gate = parts
    w_c = jnp.concatenate([w_cq, w_kr, _swap_halves(w_kr)], axis=1)
    uq = mla_w_uq[0]
    uq_rope = uq[:, :, NOPE_DIM:]
    w_q3 = jnp.concatenate([uq[:, :, :NOPE_DIM].reshape(q_lora, -1),
                            uq_rope.reshape(q_lora, -1),
                            _swap_halves(uq_rope).reshape(q_lora, -1)], axis=1)
    w_uk = mla_w_uk[0].reshape(kv_lora, -1).astype(BF16)
    w_uv = mla_w_uv[0].reshape(kv_lora, -1).astype(BF16)
    in_weights = [x.astype(BF16) for x in (w_u, w_v, w_c, w_ckv, w_qm, w_gate, w_q3)]
    in_gains = [norm_mix_g[0][None], gm_norm_g[0][None], mla_q_norm_g[0][None], mla_kv_norm_g[0][None]]
    out_weights = [x[0].astype(BF16) for x in (w_br_gm, w_br_mla, w_br_mem, w_out, ffn_w_gate, ffn_w_up, ffn_w_down)]
    gw = gm_ws[0]
    gb = jnp.broadcast_to(gm_bs[0][:, :, None], gm_bs.shape[1:] + (LANE,))

    mk, mv, mkb, mvb = _memkv(mem_prompt.reshape(batch * n_mem, d), mem_norm_g[0][None], mem_w_kv[0].astype(BF16))
    xp = x_prompt.reshape(batch * seq, d)
    cos_p, sin_p = _rope_tables(jnp.arange(seq, dtype=jnp.int32))
    o_gm, o_mem, q, ckv_p, kr_p, gates, kn, vt, krp = _inproj(xp, cos_p, sin_p, in_gains, in_weights,
                                                              (w_uk, w_uv.T, mkb, mvb, gw, gb))
    o_mla = _mix_prompt(batch, seq, q, kn, vt, krp)
    y_prompt = _merge_ffn(xp, (o_gm, o_mla, o_mem), gates, norm_ffn_g[0][None], final_norm_g[None], out_weights)

    xs = x_sample.reshape(dec_batch * dec_seq, d)
    cos_s, sin_s = _rope_tables(past + jnp.arange(dec_seq, dtype=jnp.int32))
    reps = max(INPROJ_BLOCK // dec_seq, 1)
    cos_s, sin_s = jnp.tile(cos_s, (reps, 1)), jnp.tile(sin_s, (reps, 1))
    u, vn, q, ckv_s, kr_s, qm, gates, gv_s = _inproj(xs, cos_s, sin_s, in_gains, in_weights)
    cmk = cache_mem_k[0].reshape(dec_batch * n_mem, d_mem).astype(BF16)
    cmv = cache_mem_v[0].reshape(dec_batch * n_mem, d_mem).astype(BF16)
    o_cat = _mix_sample(dec_batch, dec_seq, past, u, vn, q, qm,
                        cache_mla_ckv[0].reshape(dec_batch * past, kv_lora),
                        cache_mla_krope[0].reshape(dec_batch * past, ROPE_DIM), ckv_s, kr_s, w_uk, w_uv,
                        cmk, cmv, gw, gb)
    y_sample = _merge_ffn(xs, o_cat, gates, norm_ffn_g[0][None], final_norm_g[None], out_weights)

    return (y_prompt.reshape(batch, seq, d), y_sample.reshape(dec_batch, dec_seq, d),
            ckv_p.reshape(1, batch, seq, kv_lora), kr_p.reshape(1, batch, seq, ROPE_DIM),
            mk.reshape(1, batch, n_mem, MEM_HEADS, MEM_HEAD_DIM), mv.reshape(1, batch, n_mem, MEM_HEADS, MEM_HEAD_DIM),
            ckv_s.reshape(1, dec_batch, dec_seq, kv_lora), kr_s.reshape(1, dec_batch, dec_seq, ROPE_DIM),
            gv_s.reshape(1, dec_batch, dec_seq, d))
```
